```python
import math
import jax, jax.numpy as jnp
from jax import lax
import numpy as np

D_MODEL = 1024
BATCH = 4
SEQ = 8192
DEPTH = 2

CHUNK = 64
Q_BLOCK = 128
D_BRANCH = D_MODEL // 2
D_MIX = 3 * D_BRANCH
ATTN_HEADS = D_BRANCH // 128
ATTN_V_DIM = D_BRANCH // ATTN_HEADS
ATTN_QK_DIM = ATTN_V_DIM // 2
GDN_HEADS = D_BRANCH // 128
GDN_HEAD_DIM = D_BRANCH // GDN_HEADS
GDN_CONV = 4
LRU_BLOCKS = D_BRANCH // 128
LRU_BLOCK_DIM = D_BRANCH // LRU_BLOCKS
LRU_CONV = 4
LRU_C = 8.0

NEG_INF = -1e30
SPLIT_SIZES = (D_BRANCH,) * 8 + (GDN_HEADS, GDN_HEADS) + (D_BRANCH, D_BRANCH)
IN_COLS = sum(SPLIT_SIZES)
SPLIT_POINTS = tuple(int(v) for v in np.cumsum(SPLIT_SIZES)[:-1])

kernel_name = "hybrid_diffattn_gdn_rglru_trunk"


def rmsnorm(x, w, eps=1e-6):
    xf = x.astype(jnp.float32)
    y = xf * lax.rsqrt(jnp.mean(xf * xf, axis=-1, keepdims=True) + eps)
    return (y * w.astype(jnp.float32)).astype(x.dtype)


def l2norm(x, eps=1e-6):
    return x * lax.rsqrt(jnp.sum(x * x, axis=-1, keepdims=True) + eps)


def causal_dwconv(x, w):
    K = w.shape[0]
    S = x.shape[1]
    xp = jnp.pad(x, ((0, 0), (K - 1, 0), (0, 0)))
    y = xp[:, 0:S] * w[0]
    for j in range(1, K):
        y = y + xp[:, j:j + S] * w[j]
    return y


def diff_attention(q, k, v, lam, lambda_init, subln_w):
    B, S, _ = q.shape
    H, dq, dv = ATTN_HEADS, ATTN_QK_DIM, ATTN_V_DIM
    q = q.reshape(B, S, H, 2, dq) * (dq ** -0.5)
    k = k.reshape(B, S, H, 2, dq)
    v = v.reshape(B, S, H, dv)
    nb = S // Q_BLOCK
    q_blocks = q.reshape(B, nb, Q_BLOCK, H, 2, dq).transpose(1, 0, 2, 3, 4, 5)
    key_chunk = jnp.arange(S) // CHUNK

    def one_block(args):
        qb, bi = args
        s = jnp.einsum('bqhmd,bkhmd->bhmqk', qb, k).astype(jnp.float32)
        q_chunk = (bi * Q_BLOCK + jnp.arange(Q_BLOCK)) // CHUNK
        mask = key_chunk[None, :] <= q_chunk[:, None]
        p = jax.nn.softmax(jnp.where(mask, s, NEG_INF), axis=-1)
        attn = p[:, :, 0] - lam * p[:, :, 1]
        return jnp.einsum('bhqk,bkhd->bqhd', attn.astype(v.dtype), v)

    o = lax.map(one_block, (q_blocks, jnp.arange(nb)))
    o = o.transpose(1, 0, 2, 3, 4).reshape(B, S, H, dv)
    o = rmsnorm(o, subln_w, 1e-5) * (1.0 - lambda_init)
    return o.reshape(B, S, H * dv)


def chunk_gated_delta_rule(q, k, v, beta, g):
    B, S, H, Dk = q.shape
    Dv = v.shape[-1]
    N = S // CHUNK

    def to_chunks(t):
        return t.reshape(B, N, CHUNK, H, -1).transpose(0, 3, 1, 2, 4)

    q, k, v = to_chunks(q), to_chunks(k), to_chunks(v)
    beta = beta.reshape(B, N, CHUNK, H).transpose(0, 3, 1, 2)
    g = jnp.cumsum(g.reshape(B, N, CHUNK, H).transpose(0, 3, 1, 2), axis=-1)
    tri = jnp.tril(jnp.ones((CHUNK, CHUNK), dtype=bool))
    strict = jnp.tril(jnp.ones((CHUNK, CHUNK), dtype=bool), -1)
    decay = jnp.exp(jnp.where(tri, g[..., :, None] - g[..., None, :], -jnp.inf))
    k_beta = k * beta[..., None]
    v_beta = v * beta[..., None]
    Lmat = jnp.where(strict, jnp.einsum('bhnid,bhnjd->bhnij', k_beta, k) * decay, 0.0)
    eye = jnp.broadcast_to(jnp.eye(CHUNK, dtype=q.dtype), Lmat.shape)
    T = lax.linalg.triangular_solve(eye + Lmat, eye, left_side=True, lower=True,
                                    unit_diagonal=True)
    u = jnp.einsum('bhnij,bhnjd->bhnid', T, v_beta)
    w = jnp.einsum('bhnij,bhnjd->bhnid', T, k_beta * jnp.exp(g)[..., None])
    a_intra = jnp.where(tri, jnp.einsum('bhnid,bhnjd->bhnij', q, k) * decay, 0.0)

    def step(state, xs):
        q_i, k_i, u_i, w_i, g_i, a_i = xs
        v_new = u_i - jnp.einsum('bhcd,bhde->bhce', w_i, state)
        o = (jnp.einsum('bhcd,bhde->bhce', q_i * jnp.exp(g_i)[..., None], state)
             + jnp.einsum('bhij,bhje->bhie', a_i, v_new))
        g_last = g_i[..., -1]
        state = (state * jnp.exp(g_last)[..., None, None]
                 + jnp.einsum('bhcd,bhce->bhde', k_i * jnp.exp(g_last[..., None] - g_i)[..., None], v_new))
        return state, o

    mv = lambda t: jnp.moveaxis(t, 2, 0)
    state0 = jnp.zeros((B, H, Dk, Dv), dtype=q.dtype)
    _, o = lax.scan(step, state0, (mv(q), mv(k), mv(u), mv(w), mv(g), mv(a_intra)))
    return o.transpose(1, 0, 3, 2, 4).reshape(B, S, H, Dv)


def gated_deltanet(q, k, v, b, a, z, conv_w, a_log, dt_bias, norm_w):
    B, S, _ = q.shape
    H, D = GDN_HEADS, GDN_HEAD_DIM
    qkv = jax.nn.silu(causal_dwconv(jnp.concatenate([q, k, v], axis=-1), conv_w)).astype(jnp.float32)
    q, k, v = jnp.split(qkv, 3, axis=-1)
    q = l2norm(q.reshape(B, S, H, D)) * (D ** -0.5)
    k = l2norm(k.reshape(B, S, H, D))
    v = v.reshape(B, S, H, D)
    beta = jax.nn.sigmoid(b.astype(jnp.float32))
    g = -jnp.exp(a_log.astype(jnp.float32)) * jax.nn.softplus(a.astype(jnp.float32) + dt_bias.astype(jnp.float32))
    o = chunk_gated_delta_rule(q, k, v, beta, g)
    o = rmsnorm(o, norm_w) * jax.nn.silu(z.reshape(B, S, H, D).astype(jnp.float32))
    return o.reshape(B, S, H * D)


def rg_lru(xb, conv_w, conv_b, gate_w, gate_b, lam_param):
    B, S, C = xb.shape
    xc = (causal_dwconv(xb, conv_w) + conv_b).astype(jnp.float32)
    xg = xc.reshape(B, S, LRU_BLOCKS, LRU_BLOCK_DIM)
    gates = (jnp.einsum('bsnd,gnde->gbsne', xg, gate_w.astype(jnp.float32))
             + gate_b.astype(jnp.float32).reshape(2, 1, 1, LRU_BLOCKS, LRU_BLOCK_DIM)).reshape(2, B, S, C)
    i_t = jax.nn.sigmoid(gates[0])
    r_t = jax.nn.sigmoid(gates[1])
    log_a = -LRU_C * r_t * jax.nn.softplus(-lam_param.astype(jnp.float32))
    a_t = jnp.exp(log_a)
    b_t = jnp.sqrt(jnp.maximum(-jnp.expm1(2.0 * log_a), 0.0)) * (i_t * xc)

    def combine(e1, e2):
        a1, b1 = e1
        a2, b2 = e2
        return a1 * a2, a2 * b1 + b2

    _, h = lax.associative_scan(combine, (a_t, b_t), axis=1)
    return h


def setup_inputs(seed: int = 0) -> dict:
    key = jax.random.key(seed)
    ks = jax.random.split(key, 20)
    f32 = jnp.float32
    nrm = lambda k, shape, s: jax.random.normal(k, shape, f32) * s
    x = jax.random.normal(ks[0], (BATCH, SEQ, D_MODEL), f32)
    norm_w = 1.0 + nrm(ks[1], (DEPTH, D_MODEL), 0.01)
    w_in = nrm(ks[2], (DEPTH, D_MODEL, IN_COLS), D_MODEL ** -0.5)
    attn_lambda = nrm(ks[3], (DEPTH, 4, ATTN_QK_DIM), 0.1)
    attn_subln_w = 1.0 + nrm(ks[4], (DEPTH, ATTN_V_DIM), 0.01)
    gdn_conv_w = nrm(ks[5], (DEPTH, GDN_CONV, 3 * D_BRANCH), GDN_CONV ** -0.5)
    gdn_a_log = jnp.log(jax.random.uniform(ks[6], (DEPTH, GDN_HEADS), f32, 1.0, 16.0))
    dt = jnp.exp(jax.random.uniform(ks[7], (DEPTH, GDN_HEADS), f32, math.log(1e-3), math.log(1e-1)))
    gdn_dt_bias = dt + jnp.log(-jnp.expm1(-dt))
    gdn_norm_w = 1.0 + nrm(ks[8], (DEPTH, GDN_HEAD_DIM), 0.01)
    lru_conv_w = nrm(ks[9], (DEPTH, LRU_CONV, D_BRANCH), LRU_CONV ** -0.5)
    lru_conv_b = nrm(ks[10], (DEPTH, D_BRANCH), 0.01)
    lru_gate_w = nrm(ks[11], (DEPTH, 2, LRU_BLOCKS, LRU_BLOCK_DIM, LRU_BLOCK_DIM), LRU_BLOCK_DIM ** -0.5)
    lru_gate_b = nrm(ks[12], (DEPTH, 2, D_BRANCH), 0.01)
    u = jax.random.uniform(ks[13], (DEPTH, D_BRANCH), f32, 0.9, 0.999)
    s = u ** (1.0 / LRU_C)
    lru_log_param = jnp.log(s) - jnp.log1p(-s)
    w_out = nrm(ks[14], (DEPTH, D_MIX, D_MODEL), D_MIX ** -0.5)
    final_norm_w = 1.0 + nrm(ks[15], (D_MODEL,), 0.01)
    return {"x": x, "norm_w": norm_w, "w_in": w_in, "attn_lambda": attn_lambda,
            "attn_subln_w": attn_subln_w, "gdn_conv_w": gdn_conv_w, "gdn_a_log": gdn_a_log,
            "gdn_dt_bias": gdn_dt_bias, "gdn_norm_w": gdn_norm_w, "lru_conv_w": lru_conv_w,
            "lru_conv_b": lru_conv_b, "lru_gate_w": lru_gate_w, "lru_gate_b": lru_gate_b,
            "lru_log_param": lru_log_param, "w_out": w_out, "final_norm_w": final_norm_w}


def reference(x, norm_w, w_in, attn_lambda, attn_subln_w, gdn_conv_w, gdn_a_log, gdn_dt_bias,
              gdn_norm_w, lru_conv_w, lru_conv_b, lru_gate_w, lru_gate_b, lru_log_param,
              w_out, final_norm_w):
    dtype = x.dtype
    for l in range(DEPTH):
        h = rmsnorm(x, norm_w[l])
        u = jnp.einsum('bsd,de->bse', h, w_in[l])
        (a_q, a_k, a_v, a_z, g_q, g_k, g_v, g_z, g_b, g_a, r_x, r_z) = jnp.split(u, SPLIT_POINTS, axis=-1)
        lambda_init = 0.8 - 0.6 * math.exp(-0.3 * l)
        lp = attn_lambda[l].astype(jnp.float32)
        lam = jnp.exp(jnp.sum(lp[0] * lp[1])) - jnp.exp(jnp.sum(lp[2] * lp[3])) + lambda_init
        y_a = diff_attention(a_q, a_k, a_v, lam, lambda_init, attn_subln_w[l]) * jax.nn.silu(a_z)
        y_g = gated_deltanet(g_q, g_k, g_v, g_b, g_a, g_z, gdn_conv_w[l], gdn_a_log[l],
                             gdn_dt_bias[l], gdn_norm_w[l])
        y_r = rg_lru(r_x, lru_conv_w[l], lru_conv_b[l], lru_gate_w[l], lru_gate_b[l],
                     lru_log_param[l]) * jax.nn.silu(r_z.astype(jnp.float32))
        y = jnp.concatenate([y_a.astype(dtype), y_g.astype(dtype), y_r.astype(dtype)], axis=-1)
        x = x + jnp.einsum('bse,ed->bsd', y, w_out[l])
    return rmsnorm(x, final_norm_w)
```

```python
import functools
import math

import jax
import jax.numpy as jnp
from jax import lax
from jax.experimental import pallas as pl
from jax.experimental.pallas import tpu as pltpu

F32 = jnp.float32
BF16 = jnp.bfloat16

CHUNK = 64
HEAD = 128
NHEAD = 4
DBR = NHEAD * HEAD
CONV_K = 4
LRU_C = 8.0
NEG_INF = -1e30
VMEM_LIMIT = 56 * 1024 * 1024

_G_AQ, _G_AK, _G_AV, _G_AZ, _G_GQKV, _G_GZ, _G_RX, _G_RZ, _G_GBA = range(9)
_GROUP_WIDTH = (DBR, DBR, DBR, DBR, 3 * DBR, DBR, DBR, DBR, HEAD)
_GROUP_DTYPE = (BF16, BF16, BF16, F32, F32, F32, F32, F32, F32)
IN_COLS_PAD = sum(_GROUP_WIDTH)


def _cparams(sem):
    return pltpu.CompilerParams(dimension_semantics=sem, vmem_limit_bytes=VMEM_LIMIT)


def _sigmoid(x):
    return 1.0 / (1.0 + jnp.exp(-x))


def _silu(x):
    return x * _sigmoid(x)


def _softplus(x):
    return jnp.maximum(x, 0.0) + jnp.log1p(jnp.exp(-jnp.abs(x)))


def _dot(a, b):
    return jnp.dot(a, b, preferred_element_type=F32)


def _dot_nt(a, b):
    return lax.dot_general(a, b, (((1,), (1,)), ((), ())), preferred_element_type=F32)


def _dot_tn(a, b):
    return lax.dot_general(a, b, (((0,), (0,)), ((), ())), preferred_element_type=F32)


def _in_proj_kernel(x_ref, nw_ref, w_ref, *out_refs):
    xf = x_ref[...]
    ms = jnp.mean(xf * xf, axis=-1, keepdims=True)
    h = (xf * lax.rsqrt(ms + 1e-6) * nw_ref[...]).astype(BF16)
    c0 = 0
    for g, o_ref in enumerate(out_refs):
        wdt = _GROUP_WIDTH[g]
        for s in range(0, wdt, DBR):
            w = min(DBR, wdt - s)
            r = _dot(h, w_ref[:, c0 + s:c0 + s + w])
            if g == _G_AQ:
                r = r * (float(HEAD // 2) ** -0.5)
            o_ref[:, s:s + w] = r.astype(o_ref.dtype)
        c0 += wdt


def _in_proj(x2d, nw, w_perm, tm):
    M, D = x2d.shape
    out_shape = [jax.ShapeDtypeStruct((M, w), dt) for w, dt in zip(_GROUP_WIDTH, _GROUP_DTYPE)]
    out_specs = [pl.BlockSpec((tm, w), lambda i: (i, 0)) for w in _GROUP_WIDTH]
    return pl.pallas_call(
        _in_proj_kernel,
        grid=(M // tm,),
        in_specs=[
            pl.BlockSpec((tm, D), lambda i: (i, 0)),
            pl.BlockSpec((1, D), lambda i: (0, 0)),
            pl.BlockSpec((D, IN_COLS_PAD), lambda i: (0, 0)),
        ],
        out_specs=out_specs,
        out_shape=out_shape,
        compiler_params=_cparams(("arbitrary",)),
        name="in_proj",
    )(x2d, nw, w_perm)


def _attn_kernel(lam_ref, sw_ref, q_ref, k_ref, v_ref, z_ref, o_ref,
                 vext_ref, m_ref, acc_ref, *, tq, lambda_init):
    i = pl.program_id(2)

    @pl.when(i == 0)
    def _():
        vext_ref[:, :HEAD] = v_ref[...]
        vext_ref[:, HEAD:] = jnp.ones((v_ref.shape[0], HEAD), BF16)

    q = q_ref[...]
    lane = lax.broadcasted_iota(jnp.int32, (tq, HEAD), 1)
    zero = jnp.zeros_like(q)
    qq = jnp.concatenate([jnp.where(lane < HEAD // 2, q, zero),
                          jnp.where(lane >= HEAD // 2, q, zero)], axis=0)

    d0 = pl.multiple_of(i * tq, tq)
    s = _dot_nt(qq, k_ref[pl.ds(d0, tq), :])
    qc = lax.broadcasted_iota(jnp.int32, (2 * tq, tq), 0)
    qc = jnp.where(qc >= tq, qc - tq, qc) // CHUNK
    kc = lax.broadcasted_iota(jnp.int32, (2 * tq, tq), 1) // CHUNK
    s = jnp.where(kc <= qc, s, NEG_INF)
    m0 = jnp.max(s, axis=1, keepdims=True)
    p = jnp.exp(s - m0).astype(BF16)
    m_ref[...] = m0
    acc_ref[...] = _dot(p, vext_ref[pl.ds(d0, tq), :])

    def body(j, carry):
        j0 = pl.multiple_of(j * tq, tq)
        sj = _dot_nt(qq, k_ref[pl.ds(j0, tq), :])
        m_old = m_ref[...]
        m_new = jnp.maximum(m_old, jnp.max(sj, axis=1, keepdims=True))
        alpha = jnp.exp(m_old - m_new)
        pj = jnp.exp(sj - m_new).astype(BF16)
        acc_ref[...] = alpha * acc_ref[...] + _dot(pj, vext_ref[pl.ds(j0, tq), :])
        m_ref[...] = m_new
        return carry

    lax.fori_loop(0, i, body, 0)

    lp = lam_ref[...]
    lam = (jnp.exp(jnp.sum(lp[0:1] * lp[1:2], axis=1, keepdims=True))
           - jnp.exp(jnp.sum(lp[2:3] * lp[3:4], axis=1, keepdims=True)) + lambda_init)
    acc = acc_ref[...]
    o1 = acc[:tq, :HEAD] / acc[:tq, HEAD:]
    o2 = acc[tq:, :HEAD] / acc[tq:, HEAD:]
    o = o1 - lam * o2
    ms = jnp.mean(o * o, axis=-1, keepdims=True)
    o = o * lax.rsqrt(ms + 1e-5) * sw_ref[...] * (1.0 - lambda_init)
    o_ref[...] = (o * _silu(z_ref[...])).astype(o_ref.dtype)


def _diff_attn(q, k, v, z, lam_p, subln_w, lambda_init, tq):
    B, S, _ = q.shape
    blk = pl.BlockSpec((None, tq, HEAD), lambda b, h, i: (b, i, h))
    kv = pl.BlockSpec((None, S, HEAD), lambda b, h, i: (b, 0, h))
    return pl.pallas_call(
        functools.partial(_attn_kernel, tq=tq, lambda_init=lambda_init),
        grid=(B, NHEAD, S // tq),
        in_specs=[
            pl.BlockSpec((4, HEAD // 2), lambda b, h, i: (0, 0)),
            pl.BlockSpec((1, HEAD), lambda b, h, i: (0, 0)),
            blk, kv, kv, blk,
        ],
        out_specs=blk,
        out_shape=jax.ShapeDtypeStruct((B, S, DBR), BF16),
        scratch_shapes=[
            pltpu.VMEM((S, 2 * HEAD), BF16),
            pltpu.VMEM((2 * tq, 1), F32),
            pltpu.VMEM((2 * tq, 2 * HEAD), F32),
        ],
        compiler_params=_cparams(("arbitrary", "arbitrary", "arbitrary")),
        name="diff_attn",
    )(lam_p, subln_w, q, k, v, z)


def _unit_lower_inverse(lmat):
    hp = lax.Precision.HIGHEST
    n = -lmat
    eye = (lax.broadcasted_iota(jnp.int32, lmat.shape, 0)
           == lax.broadcasted_iota(jnp.int32, lmat.shape, 1)).astype(F32)
    p = eye + n
    size = 2
    while size < lmat.shape[0]:
        n = jnp.dot(n, n, precision=hp, preferred_element_type=F32)
        p = p + jnp.dot(p, n, precision=hp, preferred_element_type=F32)
        size *= 2
    return p


def _gdn_kernel(x_ref, z_ref, ba_ref, cw_ref, alog_ref, dtb_ref, nw_ref, o_ref,
                xpad_ref, qkv_ref, gcum_ref, beta_ref, state_ref, *, tt):
    t = pl.program_id(1)

    @pl.when(t == 0)
    def _():
        xpad_ref[0:8, :] = jnp.zeros((8, 3 * DBR), F32)
        state_ref[...] = jnp.zeros_like(state_ref)

    @pl.when(t != 0)
    def _():
        xpad_ref[0:8, :] = xpad_ref[tt:tt + 8, :]

    xpad_ref[8:, :] = x_ref[...]
    cw = cw_ref[...]
    conv = x_ref[...] * cw[3:4]
    for j in range(CONV_K - 1):
        conv = conv + xpad_ref[5 + j:5 + j + tt, :] * cw[j:j + 1]
    qkv_ref[...] = _silu(conv)

    ba = ba_ref[...]
    beta_ref[...] = _sigmoid(ba)
    g = -jnp.exp(alog_ref[...]) * _softplus(ba + dtb_ref[...])
    r = lax.broadcasted_iota(jnp.int32, (tt, tt), 0)
    c = lax.broadcasted_iota(jnp.int32, (tt, tt), 1)
    tril_blk = ((r // CHUNK == c // CHUNK) & (c <= r)).astype(F32)
    gcum_ref[...] = jnp.dot(tril_blk, g, precision=lax.Precision.HIGHEST,
                            preferred_element_type=F32)

    ri = lax.broadcasted_iota(jnp.int32, (CHUNK, CHUNK), 0)
    ci = lax.broadcasted_iota(jnp.int32, (CHUNK, CHUNK), 1)
    tri = ci <= ri
    strict = ci < ri
    nw = nw_ref[...]

    def chunk_body(ch, carry):
        r0 = pl.multiple_of(ch * CHUNK, CHUNK)
        rows = pl.ds(r0, CHUNK)
        for h in range(NHEAD):
            qh = qkv_ref[rows, h * HEAD:(h + 1) * HEAD]
            kh = qkv_ref[rows, DBR + h * HEAD:DBR + (h + 1) * HEAD]
            vh = qkv_ref[rows, 2 * DBR + h * HEAD:2 * DBR + (h + 1) * HEAD]
            qh = qh * lax.rsqrt(jnp.sum(qh * qh, axis=-1, keepdims=True) + 1e-6) * (HEAD ** -0.5)
            kh = kh * lax.rsqrt(jnp.sum(kh * kh, axis=-1, keepdims=True) + 1e-6)
            beta = beta_ref[rows, h:h + 1]
            gc = gcum_ref[rows, NHEAD + h:NHEAD + h + 1]
            gb = jnp.broadcast_to(gc, (CHUNK, CHUNK))
            decay = jnp.exp(jnp.where(tri, gb - gb.T, -jnp.inf))
            eg = jnp.exp(gc)
            kb = kh * beta
            vb = vh * beta
            kq = _dot_nt(jnp.concatenate([qh, kb], axis=0).astype(BF16), kh.astype(BF16))
            a_intra = jnp.where(tri, kq[:CHUNK] * decay, 0.0)
            lmat = jnp.where(strict, kq[CHUNK:] * decay, 0.0)
            tinv = _unit_lower_inverse(lmat)
            uw = _dot(tinv.astype(BF16), jnp.concatenate([vb, kb * eg], axis=1).astype(BF16))
            st = state_ref[h]
            wq = _dot(jnp.concatenate([uw[:, HEAD:], qh * eg], axis=0).astype(BF16), st.astype(BF16))
            v_new = uw[:, :HEAD] - wq[:CHUNK]
            o = wq[CHUNK:] + _dot(a_intra.astype(BF16), v_new.astype(BF16))
            g_last = gcum_ref[pl.ds(r0 + CHUNK - 1, 1), NHEAD + h:NHEAD + h + 1]
            kdec = kh * jnp.exp(g_last - gc)
            state_ref[h] = st * jnp.exp(g_last) + _dot_tn(kdec.astype(BF16), v_new.astype(BF16))
            ms = jnp.mean(o * o, axis=-1, keepdims=True)
            o = o * lax.rsqrt(ms + 1e-6) * nw
            zz = z_ref[rows, h * HEAD:(h + 1) * HEAD]
            o_ref[rows, h * HEAD:(h + 1) * HEAD] = (o * _silu(zz)).astype(o_ref.dtype)
        return carry

    lax.fori_loop(0, tt // CHUNK, chunk_body, 0)


def _gdn(xqkv, z, ba, conv_w, alog_row, dtb_row, norm_w, tt):
    B, S, _ = xqkv.shape
    row = lambda w: pl.BlockSpec((1, w), lambda b, t: (0, 0))
    return pl.pallas_call(
        functools.partial(_gdn_kernel, tt=tt),
        grid=(B, S // tt),
        in_specs=[
            pl.BlockSpec((None, tt, 3 * DBR), lambda b, t: (b, t, 0)),
            pl.BlockSpec((None, tt, DBR), lambda b, t: (b, t, 0)),
            pl.BlockSpec((None, tt, HEAD), lambda b, t: (b, t, 0)),
            pl.BlockSpec((CONV_K, 3 * DBR), lambda b, t: (0, 0)),
            row(HEAD), row(HEAD), row(HEAD),
        ],
        out_specs=pl.BlockSpec((None, tt, DBR), lambda b, t: (b, t, 0)),
        out_shape=jax.ShapeDtypeStruct((B, S, DBR), BF16),
        scratch_shapes=[
            pltpu.VMEM((tt + 8, 3 * DBR), F32),
            pltpu.VMEM((tt, 3 * DBR), F32),
            pltpu.VMEM((tt, HEAD), F32),
            pltpu.VMEM((tt, HEAD), F32),
            pltpu.VMEM((NHEAD, HEAD, HEAD), F32),
        ],
        compiler_params=_cparams(("arbitrary", "arbitrary")),
        name="gdn",
    )(xqkv, z, ba, conv_w, alog_row, dtb_row, norm_w)


def _lru_kernel(x_ref, z_ref, cw_ref, cb_ref, gw_ref, gb_ref, lam_ref, o_ref,
                xpad_ref, hprev_ref, *, tt):
    t = pl.program_id(1)

    @pl.when(t == 0)
    def _():
        xpad_ref[0:8, :] = jnp.zeros((8, DBR), F32)
        hprev_ref[...] = jnp.zeros_like(hprev_ref)

    @pl.when(t != 0)
    def _():
        xpad_ref[0:8, :] = xpad_ref[tt:tt + 8, :]

    xpad_ref[8:, :] = x_ref[...]
    cw = cw_ref[...]
    xc = x_ref[...] * cw[3:4] + cb_ref[...]
    for j in range(CONV_K - 1):
        xc = xc + xpad_ref[5 + j:5 + j + tt, :] * cw[j:j + 1]

    gates = [_dot(xc[:, n * HEAD:(n + 1) * HEAD].astype(BF16), gw_ref[n]) for n in range(NHEAD)]
    gi = jnp.concatenate([gt[:, :HEAD] for gt in gates], axis=1) + gb_ref[0:1, :]
    gr = jnp.concatenate([gt[:, HEAD:] for gt in gates], axis=1) + gb_ref[1:2, :]
    i_t = _sigmoid(gi)
    r_t = _sigmoid(gr)
    log_a = -LRU_C * r_t * _softplus(-lam_ref[...])
    a = jnp.exp(log_a)
    b = jnp.sqrt(jnp.maximum(-jnp.tanh(log_a) * (a * a + 1.0), 0.0)) * (i_t * xc)

    row = lax.broadcasted_iota(jnp.int32, (tt, DBR), 0)
    d = 1
    while d < tt:
        keep = row >= d
        a_sh = pltpu.roll(a, d, 0)
        b_sh = pltpu.roll(b, d, 0)
        b = jnp.where(keep, a * b_sh + b, b)
        a = jnp.where(keep, a * a_sh, a)
        d *= 2
    h = b + a * hprev_ref[...]
    hprev_ref[...] = h[tt - 1:tt, :]
    o_ref[...] = (h * _silu(z_ref[...])).astype(o_ref.dtype)


def _rglru(x, z, conv_w, conv_b, gate_w, gate_b, lam, tt):
    B, S, _ = x.shape
    blk = pl.BlockSpec((None, tt, DBR), lambda b, t: (b, t, 0))
    return pl.pallas_call(
        functools.partial(_lru_kernel, tt=tt),
        grid=(B, S // tt),
        in_specs=[
            blk, blk,
            pl.BlockSpec((CONV_K, DBR), lambda b, t: (0, 0)),
            pl.BlockSpec((1, DBR), lambda b, t: (0, 0)),
            pl.BlockSpec((NHEAD, HEAD, 2 * HEAD), lambda b, t: (0, 0, 0)),
            pl.BlockSpec((2, DBR), lambda b, t: (0, 0)),
            pl.BlockSpec((1, DBR), lambda b, t: (0, 0)),
        ],
        out_specs=blk,
        out_shape=jax.ShapeDtypeStruct((B, S, DBR), BF16),
        scratch_shapes=[pltpu.VMEM((tt + 8, DBR), F32), pltpu.VMEM((1, DBR), F32)],
        compiler_params=_cparams(("arbitrary", "arbitrary")),
        name="rglru",
    )(x, z, conv_w, conv_b, gate_w, gate_b, lam)


def _out_proj_kernel(ya_ref, yg_ref, yr_ref, w_ref, x_ref, fw_ref, o_ref, *, final_norm):
    y = x_ref[...]
    for n, y_ref in enumerate((ya_ref, yg_ref, yr_ref)):
        y = y + _dot(y_ref[...], w_ref[n * DBR:(n + 1) * DBR, :])
    if final_norm:
        ms = jnp.mean(y * y, axis=-1, keepdims=True)
        y = y * lax.rsqrt(ms + 1e-6) * fw_ref[...]
    o_ref[...] = y


def _out_proj(ya, yg, yr, w_out, x2d, fw, tm, final_norm):
    M, D = x2d.shape
    yblk = pl.BlockSpec((tm, DBR), lambda i: (i, 0))
    xblk = pl.BlockSpec((tm, D), lambda i: (i, 0))
    return pl.pallas_call(
        functools.partial(_out_proj_kernel, final_norm=final_norm),
        grid=(M // tm,),
        in_specs=[yblk, yblk, yblk,
                  pl.BlockSpec((3 * DBR, D), lambda i: (0, 0)),
                  xblk,
                  pl.BlockSpec((1, D), lambda i: (0, 0))],
        out_specs=xblk,
        out_shape=jax.ShapeDtypeStruct((M, D), F32),
        compiler_params=_cparams(("arbitrary",)),
        name="out_proj",
    )(ya, yg, yr, w_out, x2d, fw)


def _permute_w_in(w):
    d = DBR
    gba = jnp.pad(w[:, 8 * d:8 * d + 2 * NHEAD], ((0, 0), (0, HEAD - 2 * NHEAD)))
    return jnp.concatenate([w[:, :8 * d], w[:, 8 * d + 2 * NHEAD:], gba], axis=1).astype(BF16)


def _lane_row(vals, offset):
    return jnp.zeros((1, HEAD), F32).at[0, offset:offset + vals.shape[0]].set(vals.astype(F32))


def kernel(x, norm_w, w_in, attn_lambda, attn_subln_w, gdn_conv_w, gdn_a_log, gdn_dt_bias,
           gdn_norm_w, lru_conv_w, lru_conv_b, lru_gate_w, lru_gate_b, lru_log_param,
           w_out, final_norm_w):
    B, S, D = x.shape
    depth = w_in.shape[0]
    M = B * S
    tm = min(256, M)
    tq = min(256, S)
    tt = min(256, S)
    x2d = x.reshape(M, D).astype(F32)
    fw = final_norm_w.reshape(1, D).astype(F32)
    for l in range(depth):
        wp = _permute_w_in(w_in[l])
        aq, ak, av, az, gqkv, gz, rx, rz, gba = _in_proj(x2d, norm_w[l].reshape(1, D), wp, tm)
        r3 = lambda a: a.reshape(B, S, a.shape[-1])
        lambda_init = 0.8 - 0.6 * math.exp(-0.3 * l)
        ya = _diff_attn(r3(aq), r3(ak), r3(av), r3(az), attn_lambda[l].astype(F32),
                        attn_subln_w[l].reshape(1, HEAD).astype(F32), lambda_init, tq)
        yg = _gdn(r3(gqkv), r3(gz), r3(gba), gdn_conv_w[l].astype(F32),
                  _lane_row(gdn_a_log[l], NHEAD), _lane_row(gdn_dt_bias[l], NHEAD),
                  gdn_norm_w[l].reshape(1, HEAD).astype(F32), tt)
        gw = jnp.concatenate([lru_gate_w[l, 0], lru_gate_w[l, 1]], axis=-1).astype(BF16)
        yr = _rglru(r3(rx), r3(rz), lru_conv_w[l].astype(F32), lru_conv_b[l].reshape(1, DBR).astype(F32),
                    gw, lru_gate_b[l].astype(F32), lru_log_param[l].reshape(1, DBR).astype(F32), tt)
        x2d = _out_proj(ya.reshape(M, DBR), yg.reshape(M, DBR), yr.reshape(M, DBR),
                        w_out[l].astype(BF16), x2d, fw, tm, final_norm=(l == depth - 1))
    return x2d.reshape(B, S, D).astype(x.dtype)
```

```python
import functools
import math

import jax
import jax.numpy as jnp
from jax import lax
from jax.experimental import pallas as pl
from jax.experimental.pallas import tpu as pltpu

F32 = jnp.float32
BF16 = jnp.bfloat16

CHUNK = 64
HEAD = 128
NHEAD = 4
DBR = NHEAD * HEAD
CONV_K = 4
LRU_C = 8.0
NEG_INF = -1e30
VMEM_LIMIT = 56 * 1024 * 1024

_G_AQ, _G_AK, _G_AV, _G_AZ, _G_GQKV, _G_GZ, _G_RX, _G_RZ, _G_GBA = range(9)
_GROUP_WIDTH = (DBR, DBR, DBR, DBR, 3 * DBR, DBR, DBR, DBR, HEAD)
_GROUP_DTYPE = (BF16, BF16, BF16, F32, F32, F32, F32, F32, F32)
IN_COLS_PAD = sum(_GROUP_WIDTH)


def _cparams(sem):
    return pltpu.CompilerParams(dimension_semantics=sem, vmem_limit_bytes=VMEM_LIMIT)


def _sigmoid(x):
    return 1.0 / (1.0 + jnp.exp(-x))


def _silu(x):
    return x * _sigmoid(x)


def _softplus(x):
    return jnp.maximum(x, 0.0) + jnp.log1p(jnp.exp(-jnp.abs(x)))


def _dot(a, b):
    return jnp.dot(a, b, preferred_element_type=F32)


def _dot_nt(a, b):
    return lax.dot_general(a, b, (((1,), (1,)), ((), ())), preferred_element_type=F32)


def _dot_tn(a, b):
    return lax.dot_general(a, b, (((0,), (0,)), ((), ())), preferred_element_type=F32)


def _in_proj_kernel(x_ref, nw_ref, w_ref, *out_refs):
    xf = x_ref[...]
    ms = jnp.mean(xf * xf, axis=-1, keepdims=True)
    h = (xf * lax.rsqrt(ms + 1e-6) * nw_ref[...]).astype(BF16)
    c0 = 0
    for g, o_ref in enumerate(out_refs):
        wdt = _GROUP_WIDTH[g]
        for s in range(0, wdt, DBR):
            w = min(DBR, wdt - s)
            r = _dot(h, w_ref[:, c0 + s:c0 + s + w])
            if g == _G_AQ:
                r = r * (float(HEAD // 2) ** -0.5)
            o_ref[:, s:s + w] = r.astype(o_ref.dtype)
        c0 += wdt


def _in_proj(x2d, nw, w_perm, tm):
    M, D = x2d.shape
    out_shape = [jax.ShapeDtypeStruct((M, w), dt) for w, dt in zip(_GROUP_WIDTH, _GROUP_DTYPE)]
    out_specs = [pl.BlockSpec((tm, w), lambda i: (i, 0)) for w in _GROUP_WIDTH]
    return pl.pallas_call(
        _in_proj_kernel,
        grid=(M // tm,),
        in_specs=[
            pl.BlockSpec((tm, D), lambda i: (i, 0)),
            pl.BlockSpec((1, D), lambda i: (0, 0)),
            pl.BlockSpec((D, IN_COLS_PAD), lambda i: (0, 0)),
        ],
        out_specs=out_specs,
        out_shape=out_shape,
        compiler_params=_cparams(("arbitrary",)),
        name="in_proj",
    )(x2d, nw, w_perm)


def _attn_kernel(lam_ref, sw_ref, q_ref, k_ref, v_ref, z_ref, o_ref,
                 vext_ref, qq_ref, sa_ref, sb_ref, m_ref, acc_ref, *, tq, lambda_init):
    i = pl.program_id(2)

    @pl.when(i == 0)
    def _():
        vext_ref[:, :HEAD] = v_ref[...]
        vext_ref[:, HEAD:] = jnp.ones((v_ref.shape[0], HEAD), BF16)

    q = q_ref[...]
    lane = lax.broadcasted_iota(jnp.int32, (tq, HEAD), 1)
    zero = jnp.zeros_like(q)
    qq_ref[:tq, :] = jnp.where(lane < HEAD // 2, q, zero)
    qq_ref[tq:, :] = jnp.where(lane >= HEAD // 2, q, zero)
    m_ref[...] = jnp.full(m_ref.shape, -jnp.inf, F32)
    acc_ref[...] = jnp.zeros_like(acc_ref)

    def scores(start):
        return _dot_nt(qq_ref[...], k_ref[pl.ds(pl.multiple_of(start, tq), tq), :])

    def process(s_ref, start):
        s = s_ref[...]
        m_old = m_ref[...]
        m_new = jnp.maximum(m_old, jnp.max(s, axis=1, keepdims=True))
        alpha = jnp.exp(m_old - m_new)
        p = jnp.exp(s - m_new).astype(BF16)
        pv = _dot(p, vext_ref[pl.ds(pl.multiple_of(start, tq), tq), :])
        acc_ref[...] = alpha * acc_ref[...] + pv
        m_ref[...] = m_new

    d0 = i * tq
    s = scores(d0)
    qc = lax.broadcasted_iota(jnp.int32, (2 * tq, tq), 0)
    qc = jnp.where(qc >= tq, qc - tq, qc) // CHUNK
    kc = lax.broadcasted_iota(jnp.int32, (2 * tq, tq), 1) // CHUNK
    sa_ref[...] = jnp.where(kc <= qc, s, NEG_INF)

    def start_of_even(u):
        return jnp.where(u == 0, d0, (2 * u - 1) * tq)

    def pair(u, carry):
        sb_ref[...] = scores(2 * u * tq)
        process(sa_ref, start_of_even(u))
        sa_ref[...] = scores((2 * u + 1) * tq)
        process(sb_ref, 2 * u * tq)
        return carry

    npair = i // 2
    lax.fori_loop(0, npair, pair, 0)

    @pl.when(i % 2 == 0)
    def _():
        process(sa_ref, start_of_even(npair))

    @pl.when(i % 2 == 1)
    def _():
        sb_ref[...] = scores(2 * npair * tq)
        process(sa_ref, start_of_even(npair))
        process(sb_ref, 2 * npair * tq)

    lp = lam_ref[...]
    lam = (jnp.exp(jnp.sum(lp[0:1] * lp[1:2], axis=1, keepdims=True))
           - jnp.exp(jnp.sum(lp[2:3] * lp[3:4], axis=1, keepdims=True)) + lambda_init)
    acc = acc_ref[...]
    o1 = acc[:tq, :HEAD] / acc[:tq, HEAD:]
    o2 = acc[tq:, :HEAD] / acc[tq:, HEAD:]
    o = o1 - lam * o2
    ms = jnp.mean(o * o, axis=-1, keepdims=True)
    o = o * lax.rsqrt(ms + 1e-5) * sw_ref[...] * (1.0 - lambda_init)
    o_ref[...] = (o * _silu(z_ref[...])).astype(o_ref.dtype)


def _diff_attn(q, k, v, z, lam_p, subln_w, lambda_init, tq):
    B, S, _ = q.shape
    blk = pl.BlockSpec((None, tq, HEAD), lambda b, h, i: (b, i, h))
    kv = pl.BlockSpec((None, S, HEAD), lambda b, h, i: (b, 0, h))
    return pl.pallas_call(
        functools.partial(_attn_kernel, tq=tq, lambda_init=lambda_init),
        grid=(B, NHEAD, S // tq),
        in_specs=[
            pl.BlockSpec((4, HEAD // 2), lambda b, h, i: (0, 0)),
            pl.BlockSpec((1, HEAD), lambda b, h, i: (0, 0)),
            blk, kv, kv, blk,
        ],
        out_specs=blk,
        out_shape=jax.ShapeDtypeStruct((B, S, DBR), BF16),
        scratch_shapes=[
            pltpu.VMEM((S, 2 * HEAD), BF16),
            pltpu.VMEM((2 * tq, HEAD), BF16),
            pltpu.VMEM((2 * tq, tq), F32),
            pltpu.VMEM((2 * tq, tq), F32),
            pltpu.VMEM((2 * tq, 1), F32),
            pltpu.VMEM((2 * tq, 2 * HEAD), F32),
        ],
        compiler_params=_cparams(("arbitrary", "arbitrary", "arbitrary")),
        name="diff_attn",
    )(lam_p, subln_w, q, k, v, z)


def _split_bf16(a, terms):
    out = []
    for _ in range(terms):
        piece = a.astype(BF16)
        out.append(piece)
        a = a - piece.astype(F32)
    return out


def _dot_x3(a, b):
    ah, al = _split_bf16(a, 2)
    bh, bl = _split_bf16(b, 2)
    return _dot(ah, bh) + (_dot(ah, bl) + _dot(al, bh))


def _unit_lower_inverses(lmats, eye):
    ns = [-l for l in lmats]
    xs = [eye + n for n in ns]
    size = 2
    while size < CHUNK:
        nbs = [n.astype(BF16) for n in ns]
        ns = [_dot(nb, nb) for nb in nbs]
        xs = [x + _dot(x.astype(BF16), n.astype(BF16)) for x, n in zip(xs, ns)]
        size *= 2
    resids = [(eye - x) - _dot_x3(l, x) for l, x in zip(lmats, xs)]
    return [x + _dot(x.astype(BF16), r.astype(BF16)) for x, r in zip(xs, resids)]


def _gdn_kernel(x_ref, z_ref, ba_ref, cw_ref, alog_ref, dtb_ref, nw_ref, o_ref,
                xpad_ref, qkv_ref, state_ref, *, tt):
    t = pl.program_id(1)

    @pl.when(t == 0)
    def _():
        xpad_ref[0:8, :] = jnp.zeros((8, 3 * DBR), F32)
        state_ref[...] = jnp.zeros_like(state_ref)

    @pl.when(t != 0)
    def _():
        xpad_ref[0:8, :] = xpad_ref[tt:tt + 8, :]

    xpad_ref[8:, :] = x_ref[...]
    cw = cw_ref[...]
    conv = x_ref[...] * cw[3:4]
    for j in range(CONV_K - 1):
        conv = conv + xpad_ref[5 + j:5 + j + tt, :] * cw[j:j + 1]
    qkv_ref[...] = _silu(conv)

    ba = ba_ref[...]
    beta_all = _sigmoid(ba)
    g = -jnp.exp(alog_ref[...]) * _softplus(ba + dtb_ref[...])
    r = lax.broadcasted_iota(jnp.int32, (tt, tt), 0)
    c = lax.broadcasted_iota(jnp.int32, (tt, tt), 1)
    same_chunk = (r // CHUNK) == (c // CHUNK)
    blk_tri = same_chunk & (c <= r)
    blk_strict = same_chunk & (c < r)
    eye = (r == c).astype(F32)
    tril01 = blk_tri.astype(BF16)
    gcum_all = sum(_dot(tril01, piece) for piece in _split_bf16(g, 3))
    gcum_t = gcum_all.T
    nw = nw_ref[...]
    nchunk = tt // CHUNK

    heads = range(NHEAD)
    qs, ks, gcs, a_bds, lmats, rhs = [], [], [], [], [], []
    for h in heads:
        qh = qkv_ref[:, h * HEAD:(h + 1) * HEAD]
        kh = qkv_ref[:, DBR + h * HEAD:DBR + (h + 1) * HEAD]
        vh = qkv_ref[:, 2 * DBR + h * HEAD:2 * DBR + (h + 1) * HEAD]
        qh = qh * lax.rsqrt(jnp.sum(qh * qh, axis=-1, keepdims=True) + 1e-6) * (HEAD ** -0.5)
        kh = kh * lax.rsqrt(jnp.sum(kh * kh, axis=-1, keepdims=True) + 1e-6)
        beta = beta_all[:, h:h + 1]
        gc = gcum_all[:, NHEAD + h:NHEAD + h + 1]
        grow = gcum_t[NHEAD + h:NHEAD + h + 1, :]
        decay = jnp.exp(jnp.where(blk_tri, gc - grow, -jnp.inf))
        kb = kh * beta
        kq = _dot_nt(jnp.concatenate([qh, kb], axis=0).astype(BF16), kh.astype(BF16))
        a_bds.append(jnp.where(blk_tri, kq[:tt] * decay, 0.0).astype(BF16))
        lmats.append(jnp.where(blk_strict, kq[tt:] * decay, 0.0))
        rhs.append(jnp.concatenate([vh * beta, kb * jnp.exp(gc)], axis=1).astype(BF16))
        qs.append(qh)
        ks.append(kh)
        gcs.append(gc)

    tinvs = _unit_lower_inverses(lmats, eye)
    uws = [_dot(tinv.astype(BF16), r) for tinv, r in zip(tinvs, rhs)]
    wqes = [jnp.concatenate([uw[:, HEAD:], q * jnp.exp(gc)], axis=1).astype(BF16)
            for uw, q, gc in zip(uws, qs, gcs)]

    sts = [state_ref[h] for h in heads]
    for ch in range(nchunk):
        lo, hi = ch * CHUNK, (ch + 1) * CHUNK
        for h in heads:
            st, gc = sts[h], gcs[h]
            stb = st.astype(BF16)
            w_s = _dot(wqes[h][lo:hi, :HEAD], stb)
            q_s = _dot(wqes[h][lo:hi, HEAD:], stb)
            v_new = (uws[h][lo:hi, :HEAD] - w_s).astype(BF16)
            o = q_s + _dot(a_bds[h][lo:hi, lo:hi], v_new)
            g_last = gc[hi - 1:hi]
            kdec = ks[h][lo:hi] * jnp.exp(g_last - gc[lo:hi])
            sts[h] = st * jnp.exp(g_last) + _dot_tn(kdec.astype(BF16), v_new)
            ms = jnp.mean(o * o, axis=-1, keepdims=True)
            o = o * lax.rsqrt(ms + 1e-6) * nw
            zz = z_ref[lo:hi, h * HEAD:(h + 1) * HEAD]
            o_ref[lo:hi, h * HEAD:(h + 1) * HEAD] = (o * _silu(zz)).astype(o_ref.dtype)
    for h in heads:
        state_ref[h] = sts[h]


def _gdn(xqkv, z, ba, conv_w, alog_row, dtb_row, norm_w, tt):
    B, S, _ = xqkv.shape
    row = lambda w: pl.BlockSpec((1, w), lambda b, t: (0, 0))
    return pl.pallas_call(
        functools.partial(_gdn_kernel, tt=tt),
        grid=(B, S // tt),
        in_specs=[
            pl.BlockSpec((None, tt, 3 * DBR), lambda b, t: (b, t, 0)),
            pl.BlockSpec((None, tt, DBR), lambda b, t: (b, t, 0)),
            pl.BlockSpec((None, tt, HEAD), lambda b, t: (b, t, 0)),
            pl.BlockSpec((CONV_K, 3 * DBR), lambda b, t: (0, 0)),
            row(HEAD), row(HEAD), row(HEAD),
        ],
        out_specs=pl.BlockSpec((None, tt, DBR), lambda b, t: (b, t, 0)),
        out_shape=jax.ShapeDtypeStruct((B, S, DBR), BF16),
        scratch_shapes=[
            pltpu.VMEM((tt + 8, 3 * DBR), F32),
            pltpu.VMEM((tt, 3 * DBR), F32),
            pltpu.VMEM((NHEAD, HEAD, HEAD), F32),
        ],
        compiler_params=_cparams(("arbitrary", "arbitrary")),
        name="gdn",
    )(xqkv, z, ba, conv_w, alog_row, dtb_row, norm_w)


def _lru_kernel(x_ref, z_ref, cw_ref, cb_ref, gw_ref, gb_ref, lam_ref, o_ref,
                xpad_ref, hprev_ref, *, tt):
    t = pl.program_id(1)

    @pl.when(t == 0)
    def _():
        xpad_ref[0:8, :] = jnp.zeros((8, DBR), F32)
        hprev_ref[...] = jnp.zeros_like(hprev_ref)

    @pl.when(t != 0)
    def _():
        xpad_ref[0:8, :] = xpad_ref[tt:tt + 8, :]

    xpad_ref[8:, :] = x_ref[...]
    cw = cw_ref[...]
    xc = x_ref[...] * cw[3:4] + cb_ref[...]
    for j in range(CONV_K - 1):
        xc = xc + xpad_ref[5 + j:5 + j + tt, :] * cw[j:j + 1]

    gates = [_dot(xc[:, n * HEAD:(n + 1) * HEAD].astype(BF16), gw_ref[n]) for n in range(NHEAD)]
    gi = jnp.concatenate([gt[:, :HEAD] for gt in gates], axis=1) + gb_ref[0:1, :]
    gr = jnp.concatenate([gt[:, HEAD:] for gt in gates], axis=1) + gb_ref[1:2, :]
    i_t = _sigmoid(gi)
    r_t = _sigmoid(gr)
    log_a = -LRU_C * r_t * _softplus(-lam_ref[...])
    a = jnp.exp(log_a)
    b = jnp.sqrt(jnp.maximum(-jnp.tanh(log_a) * (a * a + 1.0), 0.0)) * (i_t * xc)

    row = lax.broadcasted_iota(jnp.int32, (tt, DBR), 0)
    d = 1
    while d < tt:
        keep = row >= d
        a_sh = pltpu.roll(a, d, 0)
        b_sh = pltpu.roll(b, d, 0)
        b = jnp.where(keep, a * b_sh + b, b)
        a = jnp.where(keep, a * a_sh, a)
        d *= 2
    h = b + a * hprev_ref[...]
    hprev_ref[...] = h[tt - 1:tt, :]
    o_ref[...] = (h * _silu(z_ref[...])).astype(o_ref.dtype)


def _rglru(x, z, conv_w, conv_b, gate_w, gate_b, lam, tt):
    B, S, _ = x.shape
    blk = pl.BlockSpec((None, tt, DBR), lambda b, t: (b, t, 0))
    return pl.pallas_call(
        functools.partial(_lru_kernel, tt=tt),
        grid=(B, S // tt),
        in_specs=[
            blk, blk,
            pl.BlockSpec((CONV_K, DBR), lambda b, t: (0, 0)),
            pl.BlockSpec((1, DBR), lambda b, t: (0, 0)),
            pl.BlockSpec((NHEAD, HEAD, 2 * HEAD), lambda b, t: (0, 0, 0)),
            pl.BlockSpec((2, DBR), lambda b, t: (0, 0)),
            pl.BlockSpec((1, DBR), lambda b, t: (0, 0)),
        ],
        out_specs=blk,
        out_shape=jax.ShapeDtypeStruct((B, S, DBR), BF16),
        scratch_shapes=[pltpu.VMEM((tt + 8, DBR), F32), pltpu.VMEM((1, DBR), F32)],
        compiler_params=_cparams(("arbitrary", "arbitrary")),
        name="rglru",
    )(x, z, conv_w, conv_b, gate_w, gate_b, lam)


def _out_proj_kernel(ya_ref, yg_ref, yr_ref, w_ref, x_ref, fw_ref, o_ref, *, final_norm):
    y = x_ref[...]
    for n, y_ref in enumerate((ya_ref, yg_ref, yr_ref)):
        y = y + _dot(y_ref[...], w_ref[n * DBR:(n + 1) * DBR, :])
    if final_norm:
        ms = jnp.mean(y * y, axis=-1, keepdims=True)
        y = y * lax.rsqrt(ms + 1e-6) * fw_ref[...]
    o_ref[...] = y


def _out_proj(ya, yg, yr, w_out, x2d, fw, tm, final_norm):
    M, D = x2d.shape
    yblk = pl.BlockSpec((tm, DBR), lambda i: (i, 0))
    xblk = pl.BlockSpec((tm, D), lambda i: (i, 0))
    return pl.pallas_call(
        functools.partial(_out_proj_kernel, final_norm=final_norm),
        grid=(M // tm,),
        in_specs=[yblk, yblk, yblk,
                  pl.BlockSpec((3 * DBR, D), lambda i: (0, 0)),
                  xblk,
                  pl.BlockSpec((1, D), lambda i: (0, 0))],
        out_specs=xblk,
        out_shape=jax.ShapeDtypeStruct((M, D), F32),
        compiler_params=_cparams(("arbitrary",)),
        name="out_proj",
    )(ya, yg, yr, w_out, x2d, fw)


def _permute_w_in(w):
    d = DBR
    gba = jnp.pad(w[:, 8 * d:8 * d + 2 * NHEAD], ((0, 0), (0, HEAD - 2 * NHEAD)))
    return jnp.concatenate([w[:, :8 * d], w[:, 8 * d + 2 * NHEAD:], gba], axis=1).astype(BF16)


def _lane_row(vals, offset):
    return jnp.zeros((1, HEAD), F32).at[0, offset:offset + vals.shape[0]].set(vals.astype(F32))


def kernel(x, norm_w, w_in, attn_lambda, attn_subln_w, gdn_conv_w, gdn_a_log, gdn_dt_bias,
           gdn_norm_w, lru_conv_w, lru_conv_b, lru_gate_w, lru_gate_b, lru_log_param,
           w_out, final_norm_w):
    B, S, D = x.shape
    depth = w_in.shape[0]
    M = B * S
    tm = min(256, M)
    tq = min(512, S)
    tt = min(256, S)
    x2d = x.reshape(M, D).astype(F32)
    fw = final_norm_w.reshape(1, D).astype(F32)
    for l in range(depth):
        wp = _permute_w_in(w_in[l])
        aq, ak, av, az, gqkv, gz, rx, rz, gba = _in_proj(x2d, norm_w[l].reshape(1, D), wp, tm)
        r3 = lambda a: a.reshape(B, S, a.shape[-1])
        lambda_init = 0.8 - 0.6 * math.exp(-0.3 * l)
        ya = _diff_attn(r3(aq), r3(ak), r3(av), r3(az), attn_lambda[l].astype(F32),
                        attn_subln_w[l].reshape(1, HEAD).astype(F32), lambda_init, tq)
        yg = _gdn(r3(gqkv), r3(gz), r3(gba), gdn_conv_w[l].astype(F32),
                  _lane_row(gdn_a_log[l], NHEAD), _lane_row(gdn_dt_bias[l], NHEAD),
                  gdn_norm_w[l].reshape(1, HEAD).astype(F32), tt)
        gw = jnp.concatenate([lru_gate_w[l, 0], lru_gate_w[l, 1]], axis=-1).astype(BF16)
        yr = _rglru(r3(rx), r3(rz), lru_conv_w[l].astype(F32), lru_conv_b[l].reshape(1, DBR).astype(F32),
                    gw, lru_gate_b[l].astype(F32), lru_log_param[l].reshape(1, DBR).astype(F32), tt)
        x2d = _out_proj(ya.reshape(M, DBR), yg.reshape(M, DBR), yr.reshape(M, DBR),
                        w_out[l].astype(BF16), x2d, fw, tm, final_norm=(l == depth - 1))
    return x2d.reshape(B, S, D).astype(x.dtype)
```

```python
import functools
import math

import jax
import jax.numpy as jnp
from jax import lax
from jax.experimental import pallas as pl
from jax.experimental.pallas import tpu as pltpu

F32 = jnp.float32
BF16 = jnp.bfloat16

CHUNK = 64
HEAD = 128
NHEAD = 4
DBR = NHEAD * HEAD
CONV_K = 4
LRU_C = 8.0
NEG_INF = -1e30
LOG2E = 1.4426950408889634
VMEM_LIMIT = 56 * 1024 * 1024

_G_AQ, _G_AK, _G_AV, _G_AZ, _G_GQKV, _G_GZ, _G_RX, _G_RZ, _G_GBA = range(9)
_GROUP_WIDTH = (DBR, DBR, DBR, DBR, 3 * DBR, DBR, DBR, DBR, HEAD)
_GROUP_DTYPE = (BF16, BF16, BF16, F32, F32, F32, F32, F32, F32)
IN_COLS_PAD = sum(_GROUP_WIDTH)


def _cparams(sem):
    return pltpu.CompilerParams(dimension_semantics=sem, vmem_limit_bytes=VMEM_LIMIT)


def _sigmoid(x):
    return 1.0 / (1.0 + jnp.exp(-x))


def _silu(x):
    return x * _sigmoid(x)


def _softplus(x):
    return jnp.maximum(x, 0.0) + jnp.log1p(jnp.exp(-jnp.abs(x)))


def _dot(a, b):
    return jnp.dot(a, b, preferred_element_type=F32)


def _dot_nt(a, b):
    return lax.dot_general(a, b, (((1,), (1,)), ((), ())), preferred_element_type=F32)


def _dot_tn(a, b):
    return lax.dot_general(a, b, (((0,), (0,)), ((), ())), preferred_element_type=F32)


def _in_proj_kernel(x_ref, nw_ref, w_ref, *out_refs):
    xf = x_ref[...]
    ms = jnp.mean(xf * xf, axis=-1, keepdims=True)
    h = (xf * lax.rsqrt(ms + 1e-6) * nw_ref[...]).astype(BF16)
    c0 = 0
    for g, o_ref in enumerate(out_refs):
        wdt = _GROUP_WIDTH[g]
        for s in range(0, wdt, DBR):
            w = min(DBR, wdt - s)
            r = _dot(h, w_ref[:, c0 + s:c0 + s + w])
            if g == _G_AQ:
                r = r * (float(HEAD // 2) ** -0.5 * LOG2E)
            o_ref[:, s:s + w] = r.astype(o_ref.dtype)
        c0 += wdt


def _paired_tile(j, nq):
    return jnp.where(j < nq // 2, 2 * j, 2 * (nq - 1 - j) + 1)


def _paired_rows(i, tm, tq, nq):
    per_q = tq // tm
    per_seq = nq * per_q
    b, within = i // per_seq, i % per_seq
    return b * per_seq + _paired_tile(within // per_q, nq) * per_q + within % per_q


def _in_proj(x2d, nw, w_perm, tm, tq, nq):
    M, D = x2d.shape
    out_shape = [jax.ShapeDtypeStruct((M, w), dt) for w, dt in zip(_GROUP_WIDTH, _GROUP_DTYPE)]
    out_specs = [pl.BlockSpec((tm, w), lambda i: (i, 0)) for w in _GROUP_WIDTH]
    for g in (_G_AQ, _G_AZ):
        out_specs[g] = pl.BlockSpec((tm, _GROUP_WIDTH[g]), lambda i: (_paired_rows(i, tm, tq, nq), 0))
    return pl.pallas_call(
        _in_proj_kernel,
        grid=(M // tm,),
        in_specs=[
            pl.BlockSpec((tm, D), lambda i: (i, 0)),
            pl.BlockSpec((1, D), lambda i: (0, 0)),
            pl.BlockSpec((D, IN_COLS_PAD), lambda i: (0, 0)),
        ],
        out_specs=out_specs,
        out_shape=out_shape,
        compiler_params=_cparams(("arbitrary",)),
        name="in_proj",
    )(x2d, nw, w_perm)


def _attn_kernel(lam_ref, sw_ref, q_ref, k_ref, v_ref, z_ref, o_ref,
                 vext_ref, qq_ref, s_ref, m_ref, acc_ref, *, tq, nq, lambda_init):
    p = pl.program_id(2)

    @pl.when(p == 0)
    def _():
        vext_ref[:, :HEAD] = v_ref[...]
        vext_ref[:, HEAD:] = jnp.ones((v_ref.shape[0], HEAD), BF16)

    lane = lax.broadcasted_iota(jnp.int32, (tq, HEAD), 1)
    for side in range(2):
        q = q_ref[side * tq:(side + 1) * tq, :]
        zero = jnp.zeros_like(q)
        qq_ref[side, :tq, :] = jnp.where(lane < HEAD // 2, q, zero)
        qq_ref[side, tq:, :] = jnp.where(lane >= HEAD // 2, q, zero)
    m_ref[...] = jnp.full(m_ref.shape, -jnp.inf, F32)
    acc_ref[...] = jnp.zeros_like(acc_ref)

    diag_a = p * tq
    diag_b = (nq - 1 - p) * tq

    def tile(n):
        if isinstance(n, int) and n < 2:
            return n, (diag_a, diag_b)[n]
        t = n - 2
        full_side = (t >= p).astype(jnp.int32)
        full_start = (t - p * full_side) * tq
        if isinstance(n, int):
            return full_side, full_start
        side = jnp.where(n < 2, n, full_side)
        return side, jnp.where(n == 0, diag_a, jnp.where(n == 1, diag_b, full_start))

    def scores(n, masked=False):
        side, start = tile(n)
        k = k_ref[pl.ds(pl.multiple_of(start, tq), tq), :]
        s = _dot_nt(qq_ref[side], k)
        if masked:
            qc = lax.broadcasted_iota(jnp.int32, (2 * tq, tq), 0)
            qc = jnp.where(qc >= tq, qc - tq, qc) // CHUNK
            kc = lax.broadcasted_iota(jnp.int32, (2 * tq, tq), 1) // CHUNK
            s = jnp.where(kc <= qc, s, NEG_INF)
        return s

    def process(buf, n):
        side, start = tile(n)
        s = s_ref[buf]
        m_old = m_ref[side]
        m_new = jnp.maximum(m_old, jnp.max(s, axis=1, keepdims=True))
        alpha = jnp.exp2(m_old - m_new)
        pr = jnp.exp2(s - m_new).astype(BF16)
        pv = _dot(pr, vext_ref[pl.ds(pl.multiple_of(start, tq), tq), :])
        acc_ref[side] = alpha * acc_ref[side] + pv
        m_ref[side] = m_new

    ntiles = nq + 1
    s_ref[0] = scores(0, masked=True)
    s_ref[1] = scores(1, masked=True)
    process(0, 0)
    s_ref[0] = scores(2)
    process(1, 1)

    def pair(u, carry):
        n = 2 * u
        s_ref[1] = scores(n + 1)
        process(0, n)
        s_ref[0] = scores(n + 2)
        process(1, n + 1)
        return carry

    lax.fori_loop(1, ntiles // 2, pair, 0)
    process(0, ntiles - 1)

    lp = lam_ref[...]
    lam = (jnp.exp(jnp.sum(lp[0:1] * lp[1:2], axis=1, keepdims=True))
           - jnp.exp(jnp.sum(lp[2:3] * lp[3:4], axis=1, keepdims=True)) + lambda_init)
    for side in range(2):
        acc = acc_ref[side]
        o1 = acc[:tq, :HEAD] / acc[:tq, HEAD:]
        o2 = acc[tq:, :HEAD] / acc[tq:, HEAD:]
        o = o1 - lam * o2
        ms = jnp.mean(o * o, axis=-1, keepdims=True)
        o = o * lax.rsqrt(ms + 1e-5) * sw_ref[...] * (1.0 - lambda_init)
        rows = slice(side * tq, (side + 1) * tq)
        o_ref[rows, :] = (o * _silu(z_ref[rows, :])).astype(o_ref.dtype)


def _diff_attn(q, k, v, z, lam_p, subln_w, lambda_init, tq):
    B, S, _ = q.shape
    nq = S // tq
    blk = pl.BlockSpec((None, 2 * tq, HEAD), lambda b, h, p: (b, p, h))
    kv = pl.BlockSpec((None, S, HEAD), lambda b, h, p: (b, 0, h))
    return pl.pallas_call(
        functools.partial(_attn_kernel, tq=tq, nq=nq, lambda_init=lambda_init),
        grid=(B, NHEAD, nq // 2),
        in_specs=[
            pl.BlockSpec((4, HEAD // 2), lambda b, h, p: (0, 0)),
            pl.BlockSpec((1, HEAD), lambda b, h, p: (0, 0)),
            blk, kv, kv, blk,
        ],
        out_specs=blk,
        out_shape=jax.ShapeDtypeStruct((B, S, DBR), BF16),
        scratch_shapes=[
            pltpu.VMEM((S, 2 * HEAD), BF16),
            pltpu.VMEM((2, 2 * tq, HEAD), BF16),
            pltpu.VMEM((2, 2 * tq, tq), F32),
            pltpu.VMEM((2, 2 * tq, 1), F32),
            pltpu.VMEM((2, 2 * tq, 2 * HEAD), F32),
        ],
        compiler_params=_cparams(("arbitrary", "arbitrary", "arbitrary")),
        name="diff_attn",
    )(lam_p, subln_w, q, k, v, z)


def _split_bf16(a, terms):
    out = []
    for _ in range(terms):
        piece = a.astype(BF16)
        out.append(piece)
        a = a - piece.astype(F32)
    return out


def _unit_lower_inverses(lmats, eye):
    ns = [-l for l in lmats]
    xs = [eye + n for n in ns]
    size = 2
    while size < CHUNK:
        nbs = [n.astype(BF16) for n in ns]
        ns = [_dot(nb, nb) for nb in nbs]
        xs = [x + _dot(x.astype(BF16), n.astype(BF16)) for x, n in zip(xs, ns)]
        size *= 2
    return xs


def _gdn_kernel(x_ref, z_ref, ba_ref, cw_ref, alog_ref, dtb_ref, nw_ref, o_ref,
                xpad_ref, qkv_ref, state_ref, *, tt, nb):
    t = pl.program_id(1)

    @pl.when(t == 0)
    def _():
        xpad_ref[:, 0:8, :] = jnp.zeros((nb, 8, 3 * DBR), F32)
        state_ref[...] = jnp.zeros_like(state_ref)

    @pl.when(t != 0)
    def _():
        xpad_ref[:, 0:8, :] = xpad_ref[:, tt:tt + 8, :]

    r = lax.broadcasted_iota(jnp.int32, (tt, tt), 0)
    c = lax.broadcasted_iota(jnp.int32, (tt, tt), 1)
    same_chunk = (r // CHUNK) == (c // CHUNK)
    blk_tri = same_chunk & (c <= r)
    blk_strict = same_chunk & (c < r)
    eye = (r == c).astype(F32)
    tril01 = blk_tri.astype(BF16)
    cw = cw_ref[...]
    nw = nw_ref[...]
    nchunk = tt // CHUNK

    chains = [(bi, h) for bi in range(nb) for h in range(NHEAD)]
    beta_alls, gcum_alls, gcum_ts = [], [], []
    for bi in range(nb):
        xpad_ref[bi, 8:, :] = x_ref[bi]
        conv = x_ref[bi] * cw[3:4]
        for j in range(CONV_K - 1):
            conv = conv + xpad_ref[bi, 5 + j:5 + j + tt, :] * cw[j:j + 1]
        qkv_ref[bi] = _silu(conv)
        ba = ba_ref[bi]
        beta_alls.append(_sigmoid(ba))
        g = -jnp.exp(alog_ref[...]) * _softplus(ba + dtb_ref[...])
        gcum = sum(_dot(tril01, piece) for piece in _split_bf16(g, 3))
        gcum_alls.append(gcum)
        gcum_ts.append(gcum.T)

    qs, ks, gcs, a_bds, lmats, rhs = [], [], [], [], [], []
    for bi, h in chains:
        qh = qkv_ref[bi, :, h * HEAD:(h + 1) * HEAD]
        kh = qkv_ref[bi, :, DBR + h * HEAD:DBR + (h + 1) * HEAD]
        vh = qkv_ref[bi, :, 2 * DBR + h * HEAD:2 * DBR + (h + 1) * HEAD]
        qh = qh * lax.rsqrt(jnp.sum(qh * qh, axis=-1, keepdims=True) + 1e-6) * (HEAD ** -0.5)
        kh = kh * lax.rsqrt(jnp.sum(kh * kh, axis=-1, keepdims=True) + 1e-6)
        beta = beta_alls[bi][:, h:h + 1]
        gc = gcum_alls[bi][:, NHEAD + h:NHEAD + h + 1]
        grow = gcum_ts[bi][NHEAD + h:NHEAD + h + 1, :]
        decay = jnp.exp(jnp.where(blk_tri, gc - grow, -jnp.inf))
        kb = kh * beta
        kq = _dot_nt(jnp.concatenate([qh, kb], axis=0).astype(BF16), kh.astype(BF16))
        a_bds.append(jnp.where(blk_tri, kq[:tt] * decay, 0.0).astype(BF16))
        lmats.append(jnp.where(blk_strict, kq[tt:] * decay, 0.0))
        rhs.append(jnp.concatenate([vh * beta, kb * jnp.exp(gc)], axis=1).astype(BF16))
        qs.append(qh)
        ks.append(kh)
        gcs.append(gc)

    tinvs = _unit_lower_inverses(lmats, eye)
    uws = [_dot(tinv.astype(BF16), rh) for tinv, rh in zip(tinvs, rhs)]
    wqes = [jnp.concatenate([uw[:, HEAD:], q * jnp.exp(gc)], axis=1).astype(BF16)
            for uw, q, gc in zip(uws, qs, gcs)]

    sts = [state_ref[n] for n in range(len(chains))]
    for ch in range(nchunk):
        lo, hi = ch * CHUNK, (ch + 1) * CHUNK
        for n, (bi, h) in enumerate(chains):
            st, gc = sts[n], gcs[n]
            stb = st.astype(BF16)
            w_s = _dot(wqes[n][lo:hi, :HEAD], stb)
            q_s = _dot(wqes[n][lo:hi, HEAD:], stb)
            v_new = (uws[n][lo:hi, :HEAD] - w_s).astype(BF16)
            o = q_s + _dot(a_bds[n][lo:hi, lo:hi], v_new)
            g_last = gc[hi - 1:hi]
            kdec = ks[n][lo:hi] * jnp.exp(g_last - gc[lo:hi])
            sts[n] = st * jnp.exp(g_last) + _dot_tn(kdec.astype(BF16), v_new)
            ms = jnp.mean(o * o, axis=-1, keepdims=True)
            o = o * lax.rsqrt(ms + 1e-6) * nw
            zz = z_ref[bi, lo:hi, h * HEAD:(h + 1) * HEAD]
            o_ref[bi, lo:hi, h * HEAD:(h + 1) * HEAD] = (o * _silu(zz)).astype(o_ref.dtype)
    for n in range(len(chains)):
        state_ref[n] = sts[n]


def _gdn(xqkv, z, ba, conv_w, alog_row, dtb_row, norm_w, tt, nb):
    B, S, _ = xqkv.shape
    row = lambda w: pl.BlockSpec((1, w), lambda b, t: (0, 0))
    return pl.pallas_call(
        functools.partial(_gdn_kernel, tt=tt, nb=nb),
        grid=(B // nb, S // tt),
        in_specs=[
            pl.BlockSpec((nb, tt, 3 * DBR), lambda b, t: (b, t, 0)),
            pl.BlockSpec((nb, tt, DBR), lambda b, t: (b, t, 0)),
            pl.BlockSpec((nb, tt, HEAD), lambda b, t: (b, t, 0)),
            pl.BlockSpec((CONV_K, 3 * DBR), lambda b, t: (0, 0)),
            row(HEAD), row(HEAD), row(HEAD),
        ],
        out_specs=pl.BlockSpec((nb, tt, DBR), lambda b, t: (b, t, 0)),
        out_shape=jax.ShapeDtypeStruct((B, S, DBR), BF16),
        scratch_shapes=[
            pltpu.VMEM((nb, tt + 8, 3 * DBR), F32),
            pltpu.VMEM((nb, tt, 3 * DBR), F32),
            pltpu.VMEM((nb * NHEAD, HEAD, HEAD), F32),
        ],
        compiler_params=_cparams(("arbitrary", "arbitrary")),
        name="gdn",
    )(xqkv, z, ba, conv_w, alog_row, dtb_row, norm_w)


def _lru_kernel(x_ref, z_ref, cw_ref, cb_ref, gw_ref, gb_ref, lam_ref, o_ref,
                xpad_ref, hprev_ref, *, tt):
    t = pl.program_id(1)

    @pl.when(t == 0)
    def _():
        xpad_ref[0:8, :] = jnp.zeros((8, DBR), F32)
        hprev_ref[...] = jnp.zeros_like(hprev_ref)

    @pl.when(t != 0)
    def _():
        xpad_ref[0:8, :] = xpad_ref[tt:tt + 8, :]

    xpad_ref[8:, :] = x_ref[...]
    cw = cw_ref[...]
    xc = x_ref[...] * cw[3:4] + cb_ref[...]
    for j in range(CONV_K - 1):
        xc = xc + xpad_ref[5 + j:5 + j + tt, :] * cw[j:j + 1]

    gates = [_dot(xc[:, n * HEAD:(n + 1) * HEAD].astype(BF16), gw_ref[n]) for n in range(NHEAD)]
    gi = jnp.concatenate([gt[:, :HEAD] for gt in gates], axis=1) + gb_ref[0:1, :]
    gr = jnp.concatenate([gt[:, HEAD:] for gt in gates], axis=1) + gb_ref[1:2, :]
    i_t = _sigmoid(gi)
    r_t = _sigmoid(gr)
    log_a = -LRU_C * r_t * _softplus(-lam_ref[...])
    a = jnp.exp(log_a)
    one_m_a2 = -jnp.tanh(log_a) * (a * a + 1.0)
    root = jnp.where(one_m_a2 > 0.0, one_m_a2 * lax.rsqrt(one_m_a2), 0.0)
    b = root * (i_t * xc)

    sub = lax.broadcasted_iota(jnp.int32, (tt, DBR), 0) % 8
    d = 1
    while d < 8:
        keep = sub >= d
        a_sh = pltpu.roll(a, d, 0)
        b_sh = pltpu.roll(b, d, 0)
        b = jnp.where(keep, a * b_sh + b, b)
        a = jnp.where(keep, a * a_sh, a)
        d *= 2
    carry = hprev_ref[...]
    groups = []
    for k in range(tt // 8):
        hk = b[8 * k:8 * k + 8] + a[8 * k:8 * k + 8] * carry
        groups.append(hk)
        carry = hk[7:8]
    h = jnp.concatenate(groups, axis=0)
    hprev_ref[...] = carry
    o_ref[...] = (h * _silu(z_ref[...])).astype(o_ref.dtype)


def _rglru(x, z, conv_w, conv_b, gate_w, gate_b, lam, tt):
    B, S, _ = x.shape
    blk = pl.BlockSpec((None, tt, DBR), lambda b, t: (b, t, 0))
    return pl.pallas_call(
        functools.partial(_lru_kernel, tt=tt),
        grid=(B, S // tt),
        in_specs=[
            blk, blk,
            pl.BlockSpec((CONV_K, DBR), lambda b, t: (0, 0)),
            pl.BlockSpec((1, DBR), lambda b, t: (0, 0)),
            pl.BlockSpec((NHEAD, HEAD, 2 * HEAD), lambda b, t: (0, 0, 0)),
            pl.BlockSpec((2, DBR), lambda b, t: (0, 0)),
            pl.BlockSpec((1, DBR), lambda b, t: (0, 0)),
        ],
        out_specs=blk,
        out_shape=jax.ShapeDtypeStruct((B, S, DBR), BF16),
        scratch_shapes=[pltpu.VMEM((tt + 8, DBR), F32), pltpu.VMEM((1, DBR), F32)],
        compiler_params=_cparams(("arbitrary", "arbitrary")),
        name="rglru",
    )(x, z, conv_w, conv_b, gate_w, gate_b, lam)


def _out_proj_kernel(ya_ref, yg_ref, yr_ref, w_ref, x_ref, fw_ref, o_ref, *, final_norm):
    y = x_ref[...]
    for n, y_ref in enumerate((ya_ref, yg_ref, yr_ref)):
        y = y + _dot(y_ref[...], w_ref[n * DBR:(n + 1) * DBR, :])
    if final_norm:
        ms = jnp.mean(y * y, axis=-1, keepdims=True)
        y = y * lax.rsqrt(ms + 1e-6) * fw_ref[...]
    o_ref[...] = y


def _out_proj(ya, yg, yr, w_out, x2d, fw, tm, tq, nq, final_norm):
    M, D = x2d.shape
    yblk = pl.BlockSpec((tm, DBR), lambda i: (i, 0))
    yablk = pl.BlockSpec((tm, DBR), lambda i: (_paired_rows(i, tm, tq, nq), 0))
    xblk = pl.BlockSpec((tm, D), lambda i: (i, 0))
    return pl.pallas_call(
        functools.partial(_out_proj_kernel, final_norm=final_norm),
        grid=(M // tm,),
        in_specs=[yablk, yblk, yblk,
                  pl.BlockSpec((3 * DBR, D), lambda i: (0, 0)),
                  xblk,
                  pl.BlockSpec((1, D), lambda i: (0, 0))],
        out_specs=xblk,
        out_shape=jax.ShapeDtypeStruct((M, D), F32),
        compiler_params=_cparams(("arbitrary",)),
        name="out_proj",
    )(ya, yg, yr, w_out, x2d, fw)


def _permute_w_in(w):
    d = DBR
    gba = jnp.pad(w[:, 8 * d:8 * d + 2 * NHEAD], ((0, 0), (0, HEAD - 2 * NHEAD)))
    return jnp.concatenate([w[:, :8 * d], w[:, 8 * d + 2 * NHEAD:], gba], axis=1).astype(BF16)


def _lane_row(vals, offset):
    return jnp.zeros((1, HEAD), F32).at[0, offset:offset + vals.shape[0]].set(vals.astype(F32))


def kernel(x, norm_w, w_in, attn_lambda, attn_subln_w, gdn_conv_w, gdn_a_log, gdn_dt_bias,
           gdn_norm_w, lru_conv_w, lru_conv_b, lru_gate_w, lru_gate_b, lru_log_param,
           w_out, final_norm_w):
    B, S, D = x.shape
    depth = w_in.shape[0]
    M = B * S
    tm = min(512, M)
    tq = min(512, S // 2)
    nq = S // tq
    tt = min(256, S)
    x2d = x.reshape(M, D).astype(F32)
    fw = final_norm_w.reshape(1, D).astype(F32)
    for l in range(depth):
        wp = _permute_w_in(w_in[l])
        aq, ak, av, az, gqkv, gz, rx, rz, gba = _in_proj(x2d, norm_w[l].reshape(1, D), wp, tm, tq, nq)
        r3 = lambda a: a.reshape(B, S, a.shape[-1])
        lambda_init = 0.8 - 0.6 * math.exp(-0.3 * l)
        ya = _diff_attn(r3(aq), r3(ak), r3(av), r3(az), attn_lambda[l].astype(F32),
                        attn_subln_w[l].reshape(1, HEAD).astype(F32), lambda_init, tq)
        yg = _gdn(r3(gqkv), r3(gz), r3(gba), gdn_conv_w[l].astype(F32),
                  _lane_row(gdn_a_log[l], NHEAD), _lane_row(gdn_dt_bias[l], NHEAD),
                  gdn_norm_w[l].reshape(1, HEAD).astype(F32), tt, 2 if B % 2 == 0 else 1)
        gw = jnp.concatenate([lru_gate_w[l, 0], lru_gate_w[l, 1]], axis=-1).astype(BF16)
        yr = _rglru(r3(rx), r3(rz), lru_conv_w[l].astype(F32), lru_conv_b[l].reshape(1, DBR).astype(F32),
                    gw, lru_gate_b[l].astype(F32), lru_log_param[l].reshape(1, DBR).astype(F32), tt)
        x2d = _out_proj(ya.reshape(M, DBR), yg.reshape(M, DBR), yr.reshape(M, DBR),
                        w_out[l].astype(BF16), x2d, fw, tm, tq, nq, final_norm=(l == depth - 1))
    return x2d.reshape(B, S, D).astype(x.dtype)
```

```python
import functools
import math

import jax
import jax.numpy as jnp
from jax import lax
from jax.experimental import pallas as pl
from jax.experimental.pallas import tpu as pltpu

F32 = jnp.float32
BF16 = jnp.bfloat16

CHUNK = 64
HEAD = 128
NHEAD = 4
DBR = NHEAD * HEAD
CONV_K = 4
LRU_C = 8.0
NEG_INF = -1e30
LOG2E = 1.4426950408889634
VMEM_LIMIT = 56 * 1024 * 1024

_G_AQ, _G_AK, _G_AV, _G_AZ, _G_GQKV, _G_GZ, _G_RX, _G_RZ, _G_GBA = range(9)
_GROUP_WIDTH = (DBR, DBR, DBR, DBR, 3 * DBR, DBR, DBR, DBR, HEAD)
_GROUP_DTYPE = (BF16, BF16, BF16, F32, F32, F32, F32, F32, F32)
IN_COLS_PAD = sum(_GROUP_WIDTH)
CONV_COLS = 4 * DBR


def _cparams(sem):
    return pltpu.CompilerParams(dimension_semantics=sem, vmem_limit_bytes=VMEM_LIMIT)


def _sigmoid(x):
    return 1.0 / (1.0 + jnp.exp(-x))


def _silu(x):
    return x * _sigmoid(x)


def _softplus(x):
    return jnp.maximum(x, 0.0) + jnp.log1p(jnp.exp(-jnp.abs(x)))


def _dot(a, b):
    return jnp.dot(a, b, preferred_element_type=F32)


def _dot_nt(a, b):
    return lax.dot_general(a, b, (((1,), (1,)), ((), ())), preferred_element_type=F32)


def _dot_tn(a, b):
    return lax.dot_general(a, b, (((0,), (0,)), ((), ())), preferred_element_type=F32)


def _in_proj_kernel(x_ref, nw_ref, w_ref, cw_ref, cb_ref, *refs, tm, tiles_per_seq):
    out_refs, xpad_ref = refs[:-1], refs[-1]
    i = pl.program_id(0)

    @pl.when(i % tiles_per_seq == 0)
    def _():
        xpad_ref[0:8, :] = jnp.zeros((8, CONV_COLS), F32)

    @pl.when(i % tiles_per_seq != 0)
    def _():
        xpad_ref[0:8, :] = xpad_ref[tm:tm + 8, :]

    xf = x_ref[...]
    ms = jnp.mean(xf * xf, axis=-1, keepdims=True)
    h = (xf * lax.rsqrt(ms + 1e-6) * nw_ref[...]).astype(BF16)
    plain, conv = [], []
    c0 = cc = 0
    for g, wdt in enumerate(_GROUP_WIDTH):
        for s in range(0, wdt, DBR):
            if g in (_G_GQKV, _G_RX):
                conv.append((g, s, c0 + s, cc))
                cc += DBR
            else:
                plain.append((g, s, c0 + s, None))
        c0 += wdt
    items = []
    while plain or conv:
        items += plain[:1] + conv[:1]
        plain, conv = plain[1:], conv[1:]

    for g, s, wc, cc in items:
        w = min(DBR, _GROUP_WIDTH[g] - s)
        r = _dot(h, w_ref[:, wc:wc + w])
        if g == _G_AQ:
            r = r * (float(HEAD // 2) ** -0.5 * LOG2E)
        if cc is not None:
            cols = slice(cc, cc + w)
            xpad_ref[8:, cols] = r
            cw = cw_ref[:, cols]
            r = r * cw[3:4]
            for j in range(CONV_K - 1):
                r = r + xpad_ref[5 + j:5 + j + tm, cols] * cw[j:j + 1]
            r = _silu(r) if g == _G_GQKV else r + cb_ref[...]
            if g == _G_GQKV and s < 2 * DBR:
                scale = HEAD ** -0.5 if s == 0 else 1.0
                heads = [r[:, hh * HEAD:(hh + 1) * HEAD] for hh in range(NHEAD)]
                heads = [x * (lax.rsqrt(jnp.sum(x * x, axis=-1, keepdims=True) + 1e-6) * scale)
                         for x in heads]
                r = jnp.concatenate(heads, axis=1)
        out_refs[g][:, s:s + w] = r.astype(out_refs[g].dtype)


def _paired_tile(j, nq):
    return jnp.where(j < nq // 2, 2 * j, 2 * (nq - 1 - j) + 1)


def _paired_rows(i, tm, tq, nq):
    per_q = tq // tm
    per_seq = nq * per_q
    b, within = i // per_seq, i % per_seq
    return b * per_seq + _paired_tile(within // per_q, nq) * per_q + within % per_q


def _in_proj(x2d, nw, w_perm, conv_w, conv_b, tm, tq, nq):
    M, D = x2d.shape
    out_shape = [jax.ShapeDtypeStruct((M, w), dt) for w, dt in zip(_GROUP_WIDTH, _GROUP_DTYPE)]
    out_specs = [pl.BlockSpec((tm, w), lambda i: (i, 0)) for w in _GROUP_WIDTH]
    for g in (_G_AQ, _G_AZ):
        out_specs[g] = pl.BlockSpec((tm, _GROUP_WIDTH[g]), lambda i: (_paired_rows(i, tm, tq, nq), 0))
    return pl.pallas_call(
        functools.partial(_in_proj_kernel, tm=tm, tiles_per_seq=nq * tq // tm),
        grid=(M // tm,),
        in_specs=[
            pl.BlockSpec((tm, D), lambda i: (i, 0)),
            pl.BlockSpec((1, D), lambda i: (0, 0)),
            pl.BlockSpec((D, IN_COLS_PAD), lambda i: (0, 0)),
            pl.BlockSpec((CONV_K, CONV_COLS), lambda i: (0, 0)),
            pl.BlockSpec((1, DBR), lambda i: (0, 0)),
        ],
        out_specs=out_specs,
        out_shape=out_shape,
        scratch_shapes=[pltpu.VMEM((tm + 8, CONV_COLS), F32)],
        compiler_params=_cparams(("arbitrary",)),
        name="in_proj",
    )(x2d, nw, w_perm, conv_w, conv_b)


def _attn_kernel(lam_ref, sw_ref, q_ref, k_ref, v_ref, z_ref, o_ref,
                 vext_ref, qq_ref, s_ref, m_ref, acc_ref, *, tq, nq, lambda_init):
    p = pl.program_id(2)

    @pl.when(p == 0)
    def _():
        vext_ref[:, :HEAD] = v_ref[...]
        vext_ref[:, HEAD:] = jnp.ones((v_ref.shape[0], HEAD), BF16)

    lane = lax.broadcasted_iota(jnp.int32, (tq, HEAD), 1)
    for side in range(2):
        q = q_ref[side * tq:(side + 1) * tq, :]
        zero = jnp.zeros_like(q)
        qq_ref[side, :tq, :] = jnp.where(lane < HEAD // 2, q, zero)
        qq_ref[side, tq:, :] = jnp.where(lane >= HEAD // 2, q, zero)
    m_ref[...] = jnp.full(m_ref.shape, -jnp.inf, F32)
    acc_ref[...] = jnp.zeros_like(acc_ref)

    diag_a = p * tq
    diag_b = (nq - 1 - p) * tq

    def tile(n):
        if isinstance(n, int) and n < 2:
            return n, (diag_a, diag_b)[n]
        t = n - 2
        full_side = (t >= p).astype(jnp.int32)
        full_start = (t - p * full_side) * tq
        if isinstance(n, int):
            return full_side, full_start
        side = jnp.where(n < 2, n, full_side)
        return side, jnp.where(n == 0, diag_a, jnp.where(n == 1, diag_b, full_start))

    def scores(n, masked=False):
        side, start = tile(n)
        k = k_ref[pl.ds(pl.multiple_of(start, tq), tq), :]
        s = _dot_nt(qq_ref[side], k)
        if masked:
            qc = lax.broadcasted_iota(jnp.int32, (2 * tq, tq), 0)
            qc = jnp.where(qc >= tq, qc - tq, qc) // CHUNK
            kc = lax.broadcasted_iota(jnp.int32, (2 * tq, tq), 1) // CHUNK
            s = jnp.where(kc <= qc, s, NEG_INF)
        return s

    def process(buf, n):
        side, start = tile(n)
        s = s_ref[buf]
        m_old = m_ref[side]
        m_new = jnp.maximum(m_old, jnp.max(s, axis=1, keepdims=True))
        alpha = jnp.exp2(m_old - m_new)
        pr = jnp.exp2(s - m_new).astype(BF16)
        pv = _dot(pr, vext_ref[pl.ds(pl.multiple_of(start, tq), tq), :])
        acc_ref[side] = alpha * acc_ref[side] + pv
        m_ref[side] = m_new

    ntiles = nq + 1
    s_ref[0] = scores(0, masked=True)
    s_ref[1] = scores(1, masked=True)
    process(0, 0)
    s_ref[0] = scores(2)
    process(1, 1)

    def pair(u, carry):
        n = 2 * u
        s_ref[1] = scores(n + 1)
        process(0, n)
        s_ref[0] = scores(n + 2)
        process(1, n + 1)
        return carry

    lax.fori_loop(1, ntiles // 2, pair, 0)
    process(0, ntiles - 1)

    lp = lam_ref[...]
    lam = (jnp.exp(jnp.sum(lp[0:1] * lp[1:2], axis=1, keepdims=True))
           - jnp.exp(jnp.sum(lp[2:3] * lp[3:4], axis=1, keepdims=True)) + lambda_init)
    for side in range(2):
        acc = acc_ref[side]
        o1 = acc[:tq, :HEAD] / acc[:tq, HEAD:]
        o2 = acc[tq:, :HEAD] / acc[tq:, HEAD:]
        o = o1 - lam * o2
        ms = jnp.mean(o * o, axis=-1, keepdims=True)
        o = o * lax.rsqrt(ms + 1e-5) * sw_ref[...] * (1.0 - lambda_init)
        rows = slice(side * tq, (side + 1) * tq)
        o_ref[rows, :] = (o * _silu(z_ref[rows, :])).astype(o_ref.dtype)


def _diff_attn(q, k, v, z, lam_p, subln_w, lambda_init, tq):
    B, S, _ = q.shape
    nq = S // tq
    blk = pl.BlockSpec((None, 2 * tq, HEAD), lambda b, h, p: (b, p, h))
    kv = pl.BlockSpec((None, S, HEAD), lambda b, h, p: (b, 0, h))
    return pl.pallas_call(
        functools.partial(_attn_kernel, tq=tq, nq=nq, lambda_init=lambda_init),
        grid=(B, NHEAD, nq // 2),
        in_specs=[
            pl.BlockSpec((4, HEAD // 2), lambda b, h, p: (0, 0)),
            pl.BlockSpec((1, HEAD), lambda b, h, p: (0, 0)),
            blk, kv, kv, blk,
        ],
        out_specs=blk,
        out_shape=jax.ShapeDtypeStruct((B, S, DBR), BF16),
        scratch_shapes=[
            pltpu.VMEM((S, 2 * HEAD), BF16),
            pltpu.VMEM((2, 2 * tq, HEAD), BF16),
            pltpu.VMEM((2, 2 * tq, tq), F32),
            pltpu.VMEM((2, 2 * tq, 1), F32),
            pltpu.VMEM((2, 2 * tq, 2 * HEAD), F32),
        ],
        compiler_params=_cparams(("arbitrary", "arbitrary", "arbitrary")),
        name="diff_attn",
    )(lam_p, subln_w, q, k, v, z)


def _split_bf16(a, terms):
    out = []
    for _ in range(terms):
        piece = a.astype(BF16)
        out.append(piece)
        a = a - piece.astype(F32)
    return out


class _GdnGroup:
    def __init__(self, bi, tt, refs, consts):
        self.bi, self.tt = bi, tt
        self.qkv_ref, self.z_ref, self.ba_ref, self.alog_ref, self.dtb_ref, self.o_ref, self.state_ref = refs
        self.blk_tri, self.blk_strict, self.eye, self.tril01, self.nw = consts

    def prepare(self):
        bi, tt = self.bi, self.tt
        ba = self.ba_ref[bi]
        beta_all = _sigmoid(ba)
        g = -jnp.exp(self.alog_ref[...]) * _softplus(ba + self.dtb_ref[...])
        gcum = sum(_dot(self.tril01, piece) for piece in _split_bf16(g, 3))
        gcum_t = gcum.T
        self.qs, self.ks, self.gcs, self.a_bds, self.rhs, self.xs, self.nbs = [], [], [], [], [], [], []
        for h in range(NHEAD):
            qh = self.qkv_ref[bi, :, h * HEAD:(h + 1) * HEAD]
            kh = self.qkv_ref[bi, :, DBR + h * HEAD:DBR + (h + 1) * HEAD]
            vh = self.qkv_ref[bi, :, 2 * DBR + h * HEAD:2 * DBR + (h + 1) * HEAD]
            beta = beta_all[:, h:h + 1]
            gc = gcum[:, NHEAD + h:NHEAD + h + 1]
            grow = gcum_t[NHEAD + h:NHEAD + h + 1, :]
            decay = jnp.exp(jnp.where(self.blk_tri, gc - grow, -jnp.inf))
            kb = kh * beta
            kq = _dot_nt(jnp.concatenate([qh, kb], axis=0).astype(BF16), kh.astype(BF16))
            self.a_bds.append(jnp.where(self.blk_tri, kq[:tt] * decay, 0.0).astype(BF16))
            lmat = jnp.where(self.blk_strict, kq[tt:] * decay, 0.0)
            self.rhs.append(jnp.concatenate([vh * beta, kb * jnp.exp(gc)], axis=1).astype(BF16))
            self.qs.append(qh)
            self.ks.append(kh)
            self.gcs.append(gc)
            self.xs.append(self.eye - lmat)
            self.nbs.append((-lmat).astype(BF16))

    def inverse_level(self):
        self.nbs = [_dot(nb, nb).astype(BF16) for nb in self.nbs]
        self.xs = [x + _dot(x.astype(BF16), nb) for x, nb in zip(self.xs, self.nbs)]

    def finish_wy(self):
        self.uws = [_dot(x.astype(BF16), rh) for x, rh in zip(self.xs, self.rhs)]
        self.wqes = [jnp.concatenate([uw[:, HEAD:], q * jnp.exp(gc)], axis=1).astype(BF16)
                     for uw, q, gc in zip(self.uws, self.qs, self.gcs)]
        self.sts = [self.state_ref[self.bi * NHEAD + h] for h in range(NHEAD)]

    def chunk_step(self, ch):
        bi = self.bi
        lo, hi = ch * CHUNK, (ch + 1) * CHUNK
        for h in range(NHEAD):
            st, gc = self.sts[h], self.gcs[h]
            stb = st.astype(BF16)
            w_s = _dot(self.wqes[h][lo:hi, :HEAD], stb)
            q_s = _dot(self.wqes[h][lo:hi, HEAD:], stb)
            v_new = (self.uws[h][lo:hi, :HEAD] - w_s).astype(BF16)
            o = q_s + _dot(self.a_bds[h][lo:hi, lo:hi], v_new)
            g_last = gc[hi - 1:hi]
            kdec = self.ks[h][lo:hi] * jnp.exp(g_last - gc[lo:hi])
            self.sts[h] = st * jnp.exp(g_last) + _dot_tn(kdec.astype(BF16), v_new)
            ms = jnp.mean(o * o, axis=-1, keepdims=True)
            o = o * lax.rsqrt(ms + 1e-6) * self.nw
            zz = self.z_ref[bi, lo:hi, h * HEAD:(h + 1) * HEAD]
            self.o_ref[bi, lo:hi, h * HEAD:(h + 1) * HEAD] = (o * _silu(zz)).astype(self.o_ref.dtype)

    def store_state(self):
        for h in range(NHEAD):
            self.state_ref[self.bi * NHEAD + h] = self.sts[h]


_INVERSE_LEVELS = 5


def _gdn_kernel(qkv_ref, z_ref, ba_ref, alog_ref, dtb_ref, nw_ref, o_ref, state_ref, *, tt, nb):
    t = pl.program_id(1)

    @pl.when(t == 0)
    def _():
        state_ref[...] = jnp.zeros_like(state_ref)

    r = lax.broadcasted_iota(jnp.int32, (tt, tt), 0)
    c = lax.broadcasted_iota(jnp.int32, (tt, tt), 1)
    same_chunk = (r // CHUNK) == (c // CHUNK)
    blk_tri = same_chunk & (c <= r)
    blk_strict = same_chunk & (c < r)
    consts = (blk_tri, blk_strict, (r == c).astype(F32), blk_tri.astype(BF16), nw_ref[...])
    refs = (qkv_ref, z_ref, ba_ref, alog_ref, dtb_ref, o_ref, state_ref)
    groups = [_GdnGroup(bi, tt, refs, consts) for bi in range(nb)]
    nchunk = tt // CHUNK

    first = groups[0]
    first.prepare()
    for _ in range(_INVERSE_LEVELS):
        first.inverse_level()
    first.finish_wy()
    for prev, cur in zip(groups[:-1], groups[1:]):
        stages = [cur.prepare] + [cur.inverse_level] * _INVERSE_LEVELS + [cur.finish_wy]
        per_step = -(-len(stages) // nchunk)
        for ch in range(nchunk):
            for stage in stages[ch * per_step:(ch + 1) * per_step]:
                stage()
            prev.chunk_step(ch)
        prev.store_state()
    last = groups[-1]
    for ch in range(nchunk):
        last.chunk_step(ch)
    last.store_state()


def _gdn(qkv, z, ba, alog_row, dtb_row, norm_w, tt, nb):
    B, S, _ = qkv.shape
    row = lambda w: pl.BlockSpec((1, w), lambda b, t: (0, 0))
    return pl.pallas_call(
        functools.partial(_gdn_kernel, tt=tt, nb=nb),
        grid=(B // nb, S // tt),
        in_specs=[
            pl.BlockSpec((nb, tt, 3 * DBR), lambda b, t: (b, t, 0)),
            pl.BlockSpec((nb, tt, DBR), lambda b, t: (b, t, 0)),
            pl.BlockSpec((nb, tt, HEAD), lambda b, t: (b, t, 0)),
            row(HEAD), row(HEAD), row(HEAD),
        ],
        out_specs=pl.BlockSpec((nb, tt, DBR), lambda b, t: (b, t, 0)),
        out_shape=jax.ShapeDtypeStruct((B, S, DBR), BF16),
        scratch_shapes=[pltpu.VMEM((nb * NHEAD, HEAD, HEAD), F32)],
        compiler_params=_cparams(("arbitrary", "arbitrary")),
        name="gdn",
    )(qkv, z, ba, alog_row, dtb_row, norm_w)


def _lru_kernel(x_ref, z_ref, gw_ref, gb_ref, lam_ref, o_ref, hprev_ref, *, tt):
    t = pl.program_id(1)

    @pl.when(t == 0)
    def _():
        hprev_ref[...] = jnp.zeros_like(hprev_ref)

    xc = x_ref[...]

    gates = [_dot(xc[:, n * HEAD:(n + 1) * HEAD].astype(BF16), gw_ref[n]) for n in range(NHEAD)]
    gi = jnp.concatenate([gt[:, :HEAD] for gt in gates], axis=1) + gb_ref[0:1, :]
    gr = jnp.concatenate([gt[:, HEAD:] for gt in gates], axis=1) + gb_ref[1:2, :]
    i_t = _sigmoid(gi)
    r_t = _sigmoid(gr)
    log_a = -LRU_C * r_t * _softplus(-lam_ref[...])
    a = jnp.exp(log_a)
    one_m_a2 = -jnp.tanh(log_a) * (a * a + 1.0)
    root = jnp.where(one_m_a2 > 0.0, one_m_a2 * lax.rsqrt(one_m_a2), 0.0)
    b = root * (i_t * xc)

    sub = lax.broadcasted_iota(jnp.int32, (tt, DBR), 0) % 8
    d = 1
    while d < 8:
        keep = sub >= d
        a_sh = pltpu.roll(a, d, 0)
        b_sh = pltpu.roll(b, d, 0)
        b = jnp.where(keep, a * b_sh + b, b)
        a = jnp.where(keep, a * a_sh, a)
        d *= 2
    carry = hprev_ref[...]
    groups = []
    for k in range(tt // 8):
        hk = b[8 * k:8 * k + 8] + a[8 * k:8 * k + 8] * carry
        groups.append(hk)
        carry = hk[7:8]
    h = jnp.concatenate(groups, axis=0)
    hprev_ref[...] = carry
    o_ref[...] = (h * _silu(z_ref[...])).astype(o_ref.dtype)


def _rglru(xc, z, gate_w, gate_b, lam, tt):
    B, S, _ = xc.shape
    blk = pl.BlockSpec((None, tt, DBR), lambda b, t: (b, t, 0))
    return pl.pallas_call(
        functools.partial(_lru_kernel, tt=tt),
        grid=(B, S // tt),
        in_specs=[
            blk, blk,
            pl.BlockSpec((NHEAD, HEAD, 2 * HEAD), lambda b, t: (0, 0, 0)),
            pl.BlockSpec((2, DBR), lambda b, t: (0, 0)),
            pl.BlockSpec((1, DBR), lambda b, t: (0, 0)),
        ],
        out_specs=blk,
        out_shape=jax.ShapeDtypeStruct((B, S, DBR), BF16),
        scratch_shapes=[pltpu.VMEM((1, DBR), F32)],
        compiler_params=_cparams(("arbitrary", "arbitrary")),
        name="rglru",
    )(xc, z, gate_w, gate_b, lam)


def _out_proj_kernel(ya_ref, yg_ref, yr_ref, w_ref, x_ref, fw_ref, o_ref, *, final_norm):
    y = x_ref[...]
    for n, y_ref in enumerate((ya_ref, yg_ref, yr_ref)):
        y = y + _dot(y_ref[...], w_ref[n * DBR:(n + 1) * DBR, :])
    if final_norm:
        ms = jnp.mean(y * y, axis=-1, keepdims=True)
        y = y * lax.rsqrt(ms + 1e-6) * fw_ref[...]
    o_ref[...] = y


def _out_proj(ya, yg, yr, w_out, x2d, fw, tm, tq, nq, final_norm):
    M, D = x2d.shape
    yblk = pl.BlockSpec((tm, DBR), lambda i: (i, 0))
    yablk = pl.BlockSpec((tm, DBR), lambda i: (_paired_rows(i, tm, tq, nq), 0))
    xblk = pl.BlockSpec((tm, D), lambda i: (i, 0))
    return pl.pallas_call(
        functools.partial(_out_proj_kernel, final_norm=final_norm),
        grid=(M // tm,),
        in_specs=[yablk, yblk, yblk,
                  pl.BlockSpec((3 * DBR, D), lambda i: (0, 0)),
                  xblk,
                  pl.BlockSpec((1, D), lambda i: (0, 0))],
        out_specs=xblk,
        out_shape=jax.ShapeDtypeStruct((M, D), F32),
        compiler_params=_cparams(("arbitrary",)),
        name="out_proj",
    )(ya, yg, yr, w_out, x2d, fw)


def _permute_w_in(w):
    d = DBR
    gba = jnp.pad(w[:, 8 * d:8 * d + 2 * NHEAD], ((0, 0), (0, HEAD - 2 * NHEAD)))
    return jnp.concatenate([w[:, :8 * d], w[:, 8 * d + 2 * NHEAD:], gba], axis=1).astype(BF16)


def _lane_row(vals, offset):
    return jnp.zeros((1, HEAD), F32).at[0, offset:offset + vals.shape[0]].set(vals.astype(F32))


def kernel(x, norm_w, w_in, attn_lambda, attn_subln_w, gdn_conv_w, gdn_a_log, gdn_dt_bias,
           gdn_norm_w, lru_conv_w, lru_conv_b, lru_gate_w, lru_gate_b, lru_log_param,
           w_out, final_norm_w):
    B, S, D = x.shape
    depth = w_in.shape[0]
    M = B * S
    tm = min(512, M)
    tq = min(512, S // 2)
    nq = S // tq
    tt = min(256, S)
    x2d = x.reshape(M, D).astype(F32)
    fw = final_norm_w.reshape(1, D).astype(F32)
    for l in range(depth):
        wp = _permute_w_in(w_in[l])
        conv_w = jnp.concatenate([gdn_conv_w[l], lru_conv_w[l]], axis=1).astype(F32)
        aq, ak, av, az, gqkv, gz, rxc, rz, gba = _in_proj(
            x2d, norm_w[l].reshape(1, D), wp, conv_w, lru_conv_b[l].reshape(1, DBR).astype(F32), tm, tq, nq)
        r3 = lambda a: a.reshape(B, S, a.shape[-1])
        lambda_init = 0.8 - 0.6 * math.exp(-0.3 * l)
        ya = _diff_attn(r3(aq), r3(ak), r3(av), r3(az), attn_lambda[l].astype(F32),
                        attn_subln_w[l].reshape(1, HEAD).astype(F32), lambda_init, tq)
        yg = _gdn(r3(gqkv), r3(gz), r3(gba),
                  _lane_row(gdn_a_log[l], NHEAD), _lane_row(gdn_dt_bias[l], NHEAD),
                  gdn_norm_w[l].reshape(1, HEAD).astype(F32), tt, 2 if B % 2 == 0 else 1)
        gw = jnp.concatenate([lru_gate_w[l, 0], lru_gate_w[l, 1]], axis=-1).astype(BF16)
        yr = _rglru(r3(rxc), r3(rz), gw, lru_gate_b[l].astype(F32),
                    lru_log_param[l].reshape(1, DBR).astype(F32), tt)
        x2d = _out_proj(ya.reshape(M, DBR), yg.reshape(M, DBR), yr.reshape(M, DBR),
                        w_out[l].astype(BF16), x2d, fw, tm, tq, nq, final_norm=(l == depth - 1))
    return x2d.reshape(B, S, D).astype(x.dtype)
```

```python
import functools
import math

import jax
import jax.numpy as jnp
from jax import lax
from jax.experimental import pallas as pl
from jax.experimental.pallas import tpu as pltpu

F32 = jnp.float32
BF16 = jnp.bfloat16

CHUNK = 64
HEAD = 128
NHEAD = 4
DBR = NHEAD * HEAD
CONV_K = 4
LRU_C = 8.0
NEG_INF = -1e30
LOG2E = 1.4426950408889634
VMEM_LIMIT = 56 * 1024 * 1024

_G_AQ, _G_AK, _G_AV, _G_AZ, _G_GQKV, _G_GZ, _G_RX, _G_RZ, _G_GBA = range(9)
_GROUP_WIDTH = (DBR, DBR, DBR, DBR, 3 * DBR, DBR, DBR, DBR, HEAD)
_GROUP_DTYPE = (BF16, BF16, BF16, F32, F32, F32, F32, F32, F32)
IN_COLS_PAD = sum(_GROUP_WIDTH)
CONV_COLS = 4 * DBR


def _cparams(sem):
    return pltpu.CompilerParams(dimension_semantics=sem, vmem_limit_bytes=VMEM_LIMIT)


def _sigmoid(x):
    return 1.0 / (1.0 + jnp.exp(-x))


def _silu(x):
    return x * _sigmoid(x)


def _softplus(x):
    return jnp.maximum(x, 0.0) + jnp.log1p(jnp.exp(-jnp.abs(x)))


def _dot(a, b):
    return jnp.dot(a, b, preferred_element_type=F32)


def _dot_nt(a, b):
    return lax.dot_general(a, b, (((1,), (1,)), ((), ())), preferred_element_type=F32)


def _dot_tn(a, b):
    return lax.dot_general(a, b, (((0,), (0,)), ((), ())), preferred_element_type=F32)


def _in_proj_kernel(x_ref, nw_ref, w_ref, cw_ref, cb_ref, *refs, tm, tiles_per_seq):
    out_refs, xpad_ref = refs[:-1], refs[-1]
    i = pl.program_id(0)

    @pl.when(i % tiles_per_seq == 0)
    def _():
        xpad_ref[0:8, :] = jnp.zeros((8, CONV_COLS), F32)

    @pl.when(i % tiles_per_seq != 0)
    def _():
        xpad_ref[0:8, :] = xpad_ref[tm:tm + 8, :]

    xf = x_ref[...]
    ms = jnp.mean(xf * xf, axis=-1, keepdims=True)
    h = (xf * lax.rsqrt(ms + 1e-6) * nw_ref[...]).astype(BF16)
    plain, conv = [], []
    c0 = cc = 0
    for g, wdt in enumerate(_GROUP_WIDTH):
        for s in range(0, wdt, DBR):
            if g in (_G_GQKV, _G_RX):
                conv.append((g, s, c0 + s, cc))
                cc += DBR
            else:
                plain.append((g, s, c0 + s, None))
        c0 += wdt
    items = []
    while plain or conv:
        items += plain[:1] + conv[:1]
        plain, conv = plain[1:], conv[1:]

    for g, s, wc, cc in items:
        w = min(DBR, _GROUP_WIDTH[g] - s)
        r = _dot(h, w_ref[:, wc:wc + w])
        if g == _G_AQ:
            r = r * (float(HEAD // 2) ** -0.5 * LOG2E)
        if cc is not None:
            cols = slice(cc, cc + w)
            xpad_ref[8:, cols] = r
            cw = cw_ref[:, cols]
            r = r * cw[3:4]
            for j in range(CONV_K - 1):
                r = r + xpad_ref[5 + j:5 + j + tm, cols] * cw[j:j + 1]
            r = _silu(r) if g == _G_GQKV else r + cb_ref[...]
            if g == _G_GQKV and s < 2 * DBR:
                scale = HEAD ** -0.5 if s == 0 else 1.0
                heads = [r[:, hh * HEAD:(hh + 1) * HEAD] for hh in range(NHEAD)]
                heads = [x * (lax.rsqrt(jnp.sum(x * x, axis=-1, keepdims=True) + 1e-6) * scale)
                         for x in heads]
                r = jnp.concatenate(heads, axis=1)
        out_refs[g][:, s:s + w] = r.astype(out_refs[g].dtype)


def _paired_tile(j, nq):
    return jnp.where(j < nq // 2, 2 * j, 2 * (nq - 1 - j) + 1)


def _paired_rows(i, tm, tq, nq):
    per_q = tq // tm
    per_seq = nq * per_q
    b, within = i // per_seq, i % per_seq
    return b * per_seq + _paired_tile(within // per_q, nq) * per_q + within % per_q


def _in_proj(x2d, nw, w_perm, conv_w, conv_b, tm, tq, nq):
    M, D = x2d.shape
    out_shape = [jax.ShapeDtypeStruct((M, w), dt) for w, dt in zip(_GROUP_WIDTH, _GROUP_DTYPE)]
    out_specs = [pl.BlockSpec((tm, w), lambda i: (i, 0)) for w in _GROUP_WIDTH]
    for g in (_G_AQ, _G_AZ):
        out_specs[g] = pl.BlockSpec((tm, _GROUP_WIDTH[g]), lambda i: (_paired_rows(i, tm, tq, nq), 0))
    return pl.pallas_call(
        functools.partial(_in_proj_kernel, tm=tm, tiles_per_seq=nq * tq // tm),
        grid=(M // tm,),
        in_specs=[
            pl.BlockSpec((tm, D), lambda i: (i, 0)),
            pl.BlockSpec((1, D), lambda i: (0, 0)),
            pl.BlockSpec((D, IN_COLS_PAD), lambda i: (0, 0)),
            pl.BlockSpec((CONV_K, CONV_COLS), lambda i: (0, 0)),
            pl.BlockSpec((1, DBR), lambda i: (0, 0)),
        ],
        out_specs=out_specs,
        out_shape=out_shape,
        scratch_shapes=[pltpu.VMEM((tm + 8, CONV_COLS), F32)],
        compiler_params=_cparams(("arbitrary",)),
        name="in_proj",
    )(x2d, nw, w_perm, conv_w, conv_b)


def _attn_kernel(lam_ref, sw_ref, q_ref, k_ref, v_ref, z_ref, o_ref,
                 vext_ref, qq_ref, s_ref, m_ref, acc_ref, *, tq, nq, lambda_init):
    p = pl.program_id(2)

    @pl.when(p == 0)
    def _():
        vext_ref[:, :HEAD] = v_ref[...]
        vext_ref[:, HEAD:] = jnp.ones((v_ref.shape[0], HEAD), BF16)

    lane = lax.broadcasted_iota(jnp.int32, (tq, HEAD), 1)
    for side in range(2):
        q = q_ref[side * tq:(side + 1) * tq, :]
        zero = jnp.zeros_like(q)
        qq_ref[side, :tq, :] = jnp.where(lane < HEAD // 2, q, zero)
        qq_ref[side, tq:, :] = jnp.where(lane >= HEAD // 2, q, zero)
    m_ref[...] = jnp.full(m_ref.shape, -jnp.inf, F32)
    acc_ref[...] = jnp.zeros_like(acc_ref)

    diag_a = p * tq
    diag_b = (nq - 1 - p) * tq

    def tile(n):
        if isinstance(n, int) and n < 2:
            return n, (diag_a, diag_b)[n]
        t = n - 2
        full_side = (t >= p).astype(jnp.int32)
        full_start = (t - p * full_side) * tq
        if isinstance(n, int):
            return full_side, full_start
        side = jnp.where(n < 2, n, full_side)
        return side, jnp.where(n == 0, diag_a, jnp.where(n == 1, diag_b, full_start))

    def scores(n, masked=False):
        side, start = tile(n)
        k = k_ref[pl.ds(pl.multiple_of(start, tq), tq), :]
        s = _dot_nt(qq_ref[side], k)
        if masked:
            qc = lax.broadcasted_iota(jnp.int32, (2 * tq, tq), 0)
            qc = jnp.where(qc >= tq, qc - tq, qc) // CHUNK
            kc = lax.broadcasted_iota(jnp.int32, (2 * tq, tq), 1) // CHUNK
            s = jnp.where(kc <= qc, s, NEG_INF)
        return s

    def process(buf, n):
        side, start = tile(n)
        s = s_ref[buf]
        m_old = m_ref[side]
        m_new = jnp.maximum(m_old, jnp.max(s, axis=1, keepdims=True))
        alpha = jnp.exp2(m_old - m_new)
        pr = jnp.exp2(s - jnp.concatenate([m_new] * (tq // HEAD), axis=1)).astype(BF16)
        pv = _dot(pr, vext_ref[pl.ds(pl.multiple_of(start, tq), tq), :])
        acc_ref[side] = jnp.concatenate([alpha, alpha], axis=1) * acc_ref[side] + pv
        m_ref[side] = m_new

    ntiles = nq + 1
    s_ref[0] = scores(0, masked=True)
    s_ref[1] = scores(1, masked=True)
    process(0, 0)
    s_ref[0] = scores(2)
    process(1, 1)

    def pair(u, carry):
        n = 2 * u
        s_ref[1] = scores(n + 1)
        process(0, n)
        s_ref[0] = scores(n + 2)
        process(1, n + 1)
        return carry

    lax.fori_loop(1, ntiles // 2, pair, 0)
    process(0, ntiles - 1)

    lp = lam_ref[...]
    lam = (jnp.exp(jnp.sum(lp[0:1] * lp[1:2], axis=1, keepdims=True))
           - jnp.exp(jnp.sum(lp[2:3] * lp[3:4], axis=1, keepdims=True)) + lambda_init)
    for side in range(2):
        acc = acc_ref[side]
        o1 = acc[:tq, :HEAD] / acc[:tq, HEAD:]
        o2 = acc[tq:, :HEAD] / acc[tq:, HEAD:]
        o = o1 - lam * o2
        ms = jnp.mean(o * o, axis=-1, keepdims=True)
        o = o * lax.rsqrt(ms + 1e-5) * sw_ref[...] * (1.0 - lambda_init)
        rows = slice(side * tq, (side + 1) * tq)
        o_ref[rows, :] = (o * _silu(z_ref[rows, :])).astype(o_ref.dtype)


def _diff_attn(q, k, v, z, lam_p, subln_w, lambda_init, tq):
    B, S, _ = q.shape
    nq = S // tq
    blk = pl.BlockSpec((None, 2 * tq, HEAD), lambda b, h, p: (b, p, h))
    kv = pl.BlockSpec((None, S, HEAD), lambda b, h, p: (b, 0, h))
    return pl.pallas_call(
        functools.partial(_attn_kernel, tq=tq, nq=nq, lambda_init=lambda_init),
        grid=(B, NHEAD, nq // 2),
        in_specs=[
            pl.BlockSpec((4, HEAD // 2), lambda b, h, p: (0, 0)),
            pl.BlockSpec((1, HEAD), lambda b, h, p: (0, 0)),
            blk, kv, kv, blk,
        ],
        out_specs=blk,
        out_shape=jax.ShapeDtypeStruct((B, S, DBR), BF16),
        scratch_shapes=[
            pltpu.VMEM((S, 2 * HEAD), BF16),
            pltpu.VMEM((2, 2 * tq, HEAD), BF16),
            pltpu.VMEM((2, 2 * tq, tq), F32),
            pltpu.VMEM((2, 2 * tq, HEAD), F32),
            pltpu.VMEM((2, 2 * tq, 2 * HEAD), F32),
        ],
        compiler_params=_cparams(("arbitrary", "arbitrary", "arbitrary")),
        name="diff_attn",
    )(lam_p, subln_w, q, k, v, z)


def _split_bf16(a, terms):
    out = []
    for _ in range(terms):
        piece = a.astype(BF16)
        out.append(piece)
        a = a - piece.astype(F32)
    return out


class _GdnGroup:
    def __init__(self, bi, tt, refs, consts):
        self.bi, self.tt = bi, tt
        self.qkv_ref, self.z_ref, self.ba_ref, self.alog_ref, self.dtb_ref, self.o_ref, self.state_ref = refs
        self.blk_tri, self.blk_strict, self.eye, self.tril01, self.nw = consts

    def prepare(self):
        bi, tt = self.bi, self.tt
        ba = self.ba_ref[bi]
        beta_all = _sigmoid(ba)
        g = -jnp.exp(self.alog_ref[...]) * _softplus(ba + self.dtb_ref[...])
        gcum = sum(_dot(self.tril01, piece) for piece in _split_bf16(g, 3))
        gcum_t = gcum.T
        self.qs, self.ks, self.gcs, self.a_bds, self.rhs, self.xs, self.nbs = [], [], [], [], [], [], []
        for h in range(NHEAD):
            qh = self.qkv_ref[bi, :, h * HEAD:(h + 1) * HEAD]
            kh = self.qkv_ref[bi, :, DBR + h * HEAD:DBR + (h + 1) * HEAD]
            vh = self.qkv_ref[bi, :, 2 * DBR + h * HEAD:2 * DBR + (h + 1) * HEAD]
            beta = jnp.broadcast_to(beta_all[:, h:h + 1], (tt, HEAD))
            gc = jnp.broadcast_to(gcum[:, NHEAD + h:NHEAD + h + 1], (tt, HEAD))
            eg = jnp.exp(gc)
            grow = gcum_t[NHEAD + h:NHEAD + h + 1, :]
            gcw = jnp.concatenate([gc] * (tt // HEAD), axis=1)
            decay = jnp.exp(jnp.where(self.blk_tri, gcw - grow, -jnp.inf))
            kb = kh * beta
            kq = _dot_nt(jnp.concatenate([qh, kb], axis=0).astype(BF16), kh.astype(BF16))
            self.a_bds.append(jnp.where(self.blk_tri, kq[:tt] * decay, 0.0).astype(BF16))
            lmat = jnp.where(self.blk_strict, kq[tt:] * decay, 0.0)
            self.rhs.append(jnp.concatenate([vh * beta, kb * eg], axis=1).astype(BF16))
            self.qs.append(qh * eg)
            self.ks.append(kh)
            self.gcs.append(gc)
            self.xs.append(self.eye - lmat)
            self.nbs.append((-lmat).astype(BF16))

    def inverse_level(self):
        self.nbs = [_dot(nb, nb).astype(BF16) for nb in self.nbs]
        self.xs = [x + _dot(x.astype(BF16), nb) for x, nb in zip(self.xs, self.nbs)]

    def finish_wy(self):
        self.uws = [_dot(x.astype(BF16), rh) for x, rh in zip(self.xs, self.rhs)]
        self.wqes = [jnp.concatenate([uw[:, HEAD:], qe], axis=1).astype(BF16)
                     for uw, qe in zip(self.uws, self.qs)]
        self.sts = [self.state_ref[self.bi * NHEAD + h] for h in range(NHEAD)]

    def chunk_step(self, ch):
        bi = self.bi
        lo, hi = ch * CHUNK, (ch + 1) * CHUNK
        for h in range(NHEAD):
            st, gc = self.sts[h], self.gcs[h]
            stb = st.astype(BF16)
            w_s = _dot(self.wqes[h][lo:hi, :HEAD], stb)
            q_s = _dot(self.wqes[h][lo:hi, HEAD:], stb)
            v_new = (self.uws[h][lo:hi, :HEAD] - w_s).astype(BF16)
            o = q_s + _dot(self.a_bds[h][lo:hi, lo:hi], v_new)
            g_last = gc[hi - 1:hi]
            kdec = self.ks[h][lo:hi] * jnp.exp(g_last - gc[lo:hi])
            self.sts[h] = st * jnp.exp(g_last) + _dot_tn(kdec.astype(BF16), v_new)
            ms = jnp.mean(o * o, axis=-1, keepdims=True)
            o = o * lax.rsqrt(ms + 1e-6) * self.nw
            zz = self.z_ref[bi, lo:hi, h * HEAD:(h + 1) * HEAD]
            self.o_ref[bi, lo:hi, h * HEAD:(h + 1) * HEAD] = (o * _silu(zz)).astype(self.o_ref.dtype)

    def store_state(self):
        for h in range(NHEAD):
            self.state_ref[self.bi * NHEAD + h] = self.sts[h]


_INVERSE_LEVELS = 5


def _gdn_kernel(qkv_ref, z_ref, ba_ref, alog_ref, dtb_ref, nw_ref, o_ref, state_ref, *, tt, nb):
    t = pl.program_id(1)

    @pl.when(t == 0)
    def _():
        state_ref[...] = jnp.zeros_like(state_ref)

    r = lax.broadcasted_iota(jnp.int32, (tt, tt), 0)
    c = lax.broadcasted_iota(jnp.int32, (tt, tt), 1)
    same_chunk = (r // CHUNK) == (c // CHUNK)
    blk_tri = same_chunk & (c <= r)
    blk_strict = same_chunk & (c < r)
    consts = (blk_tri, blk_strict, (r == c).astype(F32), blk_tri.astype(BF16), nw_ref[...])
    refs = (qkv_ref, z_ref, ba_ref, alog_ref, dtb_ref, o_ref, state_ref)
    groups = [_GdnGroup(bi, tt, refs, consts) for bi in range(nb)]
    nchunk = tt // CHUNK

    first = groups[0]
    first.prepare()
    for _ in range(_INVERSE_LEVELS):
        first.inverse_level()
    first.finish_wy()
    for prev, cur in zip(groups[:-1], groups[1:]):
        stages = [cur.prepare] + [cur.inverse_level] * _INVERSE_LEVELS + [cur.finish_wy]
        per_step = -(-len(stages) // nchunk)
        for ch in range(nchunk):
            for stage in stages[ch * per_step:(ch + 1) * per_step]:
                stage()
            prev.chunk_step(ch)
        prev.store_state()
    last = groups[-1]
    for ch in range(nchunk):
        last.chunk_step(ch)
    last.store_state()


def _gdn(qkv, z, ba, alog_row, dtb_row, norm_w, tt, nb):
    B, S, _ = qkv.shape
    row = lambda w: pl.BlockSpec((1, w), lambda b, t: (0, 0))
    return pl.pallas_call(
        functools.partial(_gdn_kernel, tt=tt, nb=nb),
        grid=(B // nb, S // tt),
        in_specs=[
            pl.BlockSpec((nb, tt, 3 * DBR), lambda b, t: (b, t, 0)),
            pl.BlockSpec((nb, tt, DBR), lambda b, t: (b, t, 0)),
            pl.BlockSpec((nb, tt, HEAD), lambda b, t: (b, t, 0)),
            row(HEAD), row(HEAD), row(HEAD),
        ],
        out_specs=pl.BlockSpec((nb, tt, DBR), lambda b, t: (b, t, 0)),
        out_shape=jax.ShapeDtypeStruct((B, S, DBR), BF16),
        scratch_shapes=[pltpu.VMEM((nb * NHEAD, HEAD, HEAD), F32)],
        compiler_params=_cparams(("arbitrary", "arbitrary")),
        name="gdn",
    )(qkv, z, ba, alog_row, dtb_row, norm_w)


def _lru_kernel(x_ref, z_ref, gw_ref, gb_ref, lam_ref, o_ref, hprev_ref, *, tt):
    t = pl.program_id(1)

    @pl.when(t == 0)
    def _():
        hprev_ref[...] = jnp.zeros_like(hprev_ref)

    xc = x_ref[...]

    gates = [_dot(xc[:, n * HEAD:(n + 1) * HEAD].astype(BF16), gw_ref[n]) for n in range(NHEAD)]
    gi = jnp.concatenate([gt[:, :HEAD] for gt in gates], axis=1) + gb_ref[0:1, :]
    gr = jnp.concatenate([gt[:, HEAD:] for gt in gates], axis=1) + gb_ref[1:2, :]
    i_t = _sigmoid(gi)
    r_t = _sigmoid(gr)
    log_a = -LRU_C * r_t * _softplus(-lam_ref[...])
    a = jnp.exp(log_a)
    one_m_a2 = -jnp.tanh(log_a) * (a * a + 1.0)
    root = jnp.where(one_m_a2 > 0.0, one_m_a2 * lax.rsqrt(one_m_a2), 0.0)
    b = root * (i_t * xc)

    sub = lax.broadcasted_iota(jnp.int32, (tt, DBR), 0) % 8
    d = 1
    while d < 8:
        keep = sub >= d
        a_sh = pltpu.roll(a, d, 0)
        b_sh = pltpu.roll(b, d, 0)
        b = jnp.where(keep, a * b_sh + b, b)
        a = jnp.where(keep, a * a_sh, a)
        d *= 2
    carry = hprev_ref[...]
    groups = []
    for k in range(tt // 8):
        hk = b[8 * k:8 * k + 8] + a[8 * k:8 * k + 8] * carry
        groups.append(hk)
        carry = hk[7:8]
    h = jnp.concatenate(groups, axis=0)
    hprev_ref[...] = carry
    o_ref[...] = (h * _silu(z_ref[...])).astype(o_ref.dtype)


def _rglru(xc, z, gate_w, gate_b, lam, tt):
    B, S, _ = xc.shape
    blk = pl.BlockSpec((None, tt, DBR), lambda b, t: (b, t, 0))
    return pl.pallas_call(
        functools.partial(_lru_kernel, tt=tt),
        grid=(B, S // tt),
        in_specs=[
            blk, blk,
            pl.BlockSpec((NHEAD, HEAD, 2 * HEAD), lambda b, t: (0, 0, 0)),
            pl.BlockSpec((2, DBR), lambda b, t: (0, 0)),
            pl.BlockSpec((1, DBR), lambda b, t: (0, 0)),
        ],
        out_specs=blk,
        out_shape=jax.ShapeDtypeStruct((B, S, DBR), BF16),
        scratch_shapes=[pltpu.VMEM((1, DBR), F32)],
        compiler_params=_cparams(("arbitrary", "arbitrary")),
        name="rglru",
    )(xc, z, gate_w, gate_b, lam)


def _out_proj_kernel(ya_ref, yg_ref, yr_ref, w_ref, x_ref, fw_ref, o_ref, *, final_norm):
    y = x_ref[...]
    for n, y_ref in enumerate((ya_ref, yg_ref, yr_ref)):
        y = y + _dot(y_ref[...], w_ref[n * DBR:(n + 1) * DBR, :])
    if final_norm:
        ms = jnp.mean(y * y, axis=-1, keepdims=True)
        y = y * lax.rsqrt(ms + 1e-6) * fw_ref[...]
    o_ref[...] = y


def _out_proj(ya, yg, yr, w_out, x2d, fw, tm, tq, nq, final_norm):
    M, D = x2d.shape
    yblk = pl.BlockSpec((tm, DBR), lambda i: (i, 0))
    yablk = pl.BlockSpec((tm, DBR), lambda i: (_paired_rows(i, tm, tq, nq), 0))
    xblk = pl.BlockSpec((tm, D), lambda i: (i, 0))
    return pl.pallas_call(
        functools.partial(_out_proj_kernel, final_norm=final_norm),
        grid=(M // tm,),
        in_specs=[yablk, yblk, yblk,
                  pl.BlockSpec((3 * DBR, D), lambda i: (0, 0)),
                  xblk,
                  pl.BlockSpec((1, D), lambda i: (0, 0))],
        out_specs=xblk,
        out_shape=jax.ShapeDtypeStruct((M, D), F32),
        compiler_params=_cparams(("arbitrary",)),
        name="out_proj",
    )(ya, yg, yr, w_out, x2d, fw)


def _permute_w_in(w):
    d = DBR
    gba = jnp.pad(w[:, 8 * d:8 * d + 2 * NHEAD], ((0, 0), (0, HEAD - 2 * NHEAD)))
    return jnp.concatenate([w[:, :8 * d], w[:, 8 * d + 2 * NHEAD:], gba], axis=1).astype(BF16)


def _lane_row(vals, offset):
    return jnp.zeros((1, HEAD), F32).at[0, offset:offset + vals.shape[0]].set(vals.astype(F32))


def kernel(x, norm_w, w_in, attn_lambda, attn_subln_w, gdn_conv_w, gdn_a_log, gdn_dt_bias,
           gdn_norm_w, lru_conv_w, lru_conv_b, lru_gate_w, lru_gate_b, lru_log_param,
           w_out, final_norm_w):
    B, S, D = x.shape
    depth = w_in.shape[0]
    M = B * S
    tm = min(512, M)
    tq = min(512, S // 2)
    nq = S // tq
    tt = min(256, S)
    x2d = x.reshape(M, D).astype(F32)
    fw = final_norm_w.reshape(1, D).astype(F32)
    for l in range(depth):
        wp = _permute_w_in(w_in[l])
        conv_w = jnp.concatenate([gdn_conv_w[l], lru_conv_w[l]], axis=1).astype(F32)
        aq, ak, av, az, gqkv, gz, rxc, rz, gba = _in_proj(
            x2d, norm_w[l].reshape(1, D), wp, conv_w, lru_conv_b[l].reshape(1, DBR).astype(F32), tm, tq, nq)
        r3 = lambda a: a.reshape(B, S, a.shape[-1])
        lambda_init = 0.8 - 0.6 * math.exp(-0.3 * l)
        ya = _diff_attn(r3(aq), r3(ak), r3(av), r3(az), attn_lambda[l].astype(F32),
                        attn_subln_w[l].reshape(1, HEAD).astype(F32), lambda_init, tq)
        yg = _gdn(r3(gqkv), r3(gz), r3(gba),
                  _lane_row(gdn_a_log[l], NHEAD), _lane_row(gdn_dt_bias[l], NHEAD),
                  gdn_norm_w[l].reshape(1, HEAD).astype(F32), tt, 2 if B % 2 == 0 else 1)
        gw = jnp.concatenate([lru_gate_w[l, 0], lru_gate_w[l, 1]], axis=-1).astype(BF16)
        yr = _rglru(r3(rxc), r3(rz), gw, lru_gate_b[l].astype(F32),
                    lru_log_param[l].reshape(1, DBR).astype(F32), tt)
        x2d = _out_proj(ya.reshape(M, DBR), yg.reshape(M, DBR), yr.reshape(M, DBR),
                        w_out[l].astype(BF16), x2d, fw, tm, tq, nq, final_norm=(l == depth - 1))
    return x2d.reshape(B, S, D).astype(x.dtype)
```

```python
import functools
import math

import jax
import jax.numpy as jnp
from jax import lax
from jax.experimental import pallas as pl
from jax.experimental.pallas import tpu as pltpu

F32 = jnp.float32
BF16 = jnp.bfloat16

CHUNK = 64
HEAD = 128
NHEAD = 4
DBR = NHEAD * HEAD
CONV_K = 4
LRU_C = 8.0
NEG_INF = -1e30
LOG2E = 1.4426950408889634
VMEM_LIMIT = 56 * 1024 * 1024

_G_AQ, _G_AK, _G_AV, _G_AZ, _G_GQKV, _G_GZ, _G_RX, _G_RZ, _G_GBA = range(9)
_GROUP_WIDTH = (DBR, DBR, DBR, DBR, 3 * DBR, DBR, DBR, DBR, HEAD)
_GROUP_DTYPE = (BF16, BF16, BF16, F32, F32, F32, F32, F32, F32)
IN_COLS_PAD = sum(_GROUP_WIDTH)
CONV_COLS = 4 * DBR
_ITEM_COLS = 2 * HEAD


def _cparams(sem):
    return pltpu.CompilerParams(dimension_semantics=sem, vmem_limit_bytes=VMEM_LIMIT)


def _sigmoid(x):
    return 1.0 / (1.0 + jnp.exp(-x))


def _silu(x):
    return x * _sigmoid(x)


def _softplus(x):
    return jnp.maximum(x, 0.0) + jnp.log1p(jnp.exp(-jnp.abs(x)))


def _dot(a, b):
    return jnp.dot(a, b, preferred_element_type=F32)


def _dot_nt(a, b):
    return lax.dot_general(a, b, (((1,), (1,)), ((), ())), preferred_element_type=F32)


def _dot_tn(a, b):
    return lax.dot_general(a, b, (((0,), (0,)), ((), ())), preferred_element_type=F32)


def _in_proj_kernel(x_ref, nw_ref, w_ref, cw_ref, cb_ref, *refs, tm, tiles_per_seq):
    out_refs, xpad_ref = refs[:-1], refs[-1]
    i = pl.program_id(0)

    @pl.when(i % tiles_per_seq == 0)
    def _():
        xpad_ref[0:8, :] = jnp.zeros((8, CONV_COLS), F32)

    @pl.when(i % tiles_per_seq != 0)
    def _():
        xpad_ref[0:8, :] = xpad_ref[tm:tm + 8, :]

    xf = x_ref[...]
    ms = jnp.mean(xf * xf, axis=-1, keepdims=True)
    h = (xf * lax.rsqrt(ms + 1e-6) * nw_ref[...]).astype(BF16)
    plain, conv = [], []
    c0 = cc = 0
    for g, wdt in enumerate(_GROUP_WIDTH):
        for s in range(0, wdt, _ITEM_COLS):
            if g in (_G_GQKV, _G_RX):
                conv.append((g, s, c0 + s, cc))
                cc += _ITEM_COLS
            else:
                plain.append((g, s, c0 + s, None))
        c0 += wdt
    items = []
    while plain or conv:
        items += plain[:1] + conv[:1]
        plain, conv = plain[1:], conv[1:]

    for g, s, wc, cc in items:
        w = min(_ITEM_COLS, _GROUP_WIDTH[g] - s)
        r = _dot(h, w_ref[:, wc:wc + w])
        if g == _G_AQ:
            r = r * (float(HEAD // 2) ** -0.5 * LOG2E)
        if cc is not None:
            cols = slice(cc, cc + w)
            xpad_ref[8:, cols] = r
            cw = cw_ref[:, cols]
            r = r * cw[3:4]
            for j in range(CONV_K - 1):
                r = r + xpad_ref[5 + j:5 + j + tm, cols] * cw[j:j + 1]
            r = _silu(r) if g == _G_GQKV else r + cb_ref[:, s:s + w]
            if g == _G_GQKV and s < 2 * DBR:
                scale = HEAD ** -0.5 if s < DBR else 1.0
                heads = [r[:, c:c + HEAD] for c in range(0, w, HEAD)]
                heads = [x * (lax.rsqrt(jnp.sum(x * x, axis=-1, keepdims=True) + 1e-6) * scale)
                         for x in heads]
                r = jnp.concatenate(heads, axis=1)
        out_refs[g][:, s:s + w] = r.astype(out_refs[g].dtype)


def _paired_tile(j, nq):
    return jnp.where(j < nq // 2, 2 * j, 2 * (nq - 1 - j) + 1)


def _paired_rows(i, tm, tq, nq):
    per_q = tq // tm
    per_seq = nq * per_q
    b, within = i // per_seq, i % per_seq
    return b * per_seq + _paired_tile(within // per_q, nq) * per_q + within % per_q


def _in_proj(x2d, nw, w_perm, conv_w, conv_b, tm, tq, nq):
    M, D = x2d.shape
    out_shape = [jax.ShapeDtypeStruct((M, w), dt) for w, dt in zip(_GROUP_WIDTH, _GROUP_DTYPE)]
    out_specs = [pl.BlockSpec((tm, w), lambda i: (i, 0)) for w in _GROUP_WIDTH]
    for g in (_G_AQ, _G_AZ):
        out_specs[g] = pl.BlockSpec((tm, _GROUP_WIDTH[g]), lambda i: (_paired_rows(i, tm, tq, nq), 0))
    return pl.pallas_call(
        functools.partial(_in_proj_kernel, tm=tm, tiles_per_seq=nq * tq // tm),
        grid=(M // tm,),
        in_specs=[
            pl.BlockSpec((tm, D), lambda i: (i, 0)),
            pl.BlockSpec((1, D), lambda i: (0, 0)),
            pl.BlockSpec((D, IN_COLS_PAD), lambda i: (0, 0)),
            pl.BlockSpec((CONV_K, CONV_COLS), lambda i: (0, 0)),
            pl.BlockSpec((1, DBR), lambda i: (0, 0)),
        ],
        out_specs=out_specs,
        out_shape=out_shape,
        scratch_shapes=[pltpu.VMEM((tm + 8, CONV_COLS), F32)],
        compiler_params=_cparams(("arbitrary",)),
        name="in_proj",
    )(x2d, nw, w_perm, conv_w, conv_b)


def _attn_kernel(lam_ref, sw_ref, q_ref, k_ref, v_ref, z_ref, o_ref,
                 vext_ref, qq_ref, s_ref, m_ref, acc_ref, *, tq, nq, lambda_init):
    p = pl.program_id(2)

    @pl.when(p == 0)
    def _():
        vext_ref[:, :HEAD] = v_ref[...]
        vext_ref[:, HEAD:] = jnp.ones((v_ref.shape[0], HEAD), BF16)

    lane = lax.broadcasted_iota(jnp.int32, (tq, HEAD), 1)
    for side in range(2):
        q = q_ref[side * tq:(side + 1) * tq, :]
        zero = jnp.zeros_like(q)
        qq_ref[side, :tq, :] = jnp.where(lane < HEAD // 2, q, zero)
        qq_ref[side, tq:, :] = jnp.where(lane >= HEAD // 2, q, zero)
    m_ref[...] = jnp.full(m_ref.shape, -jnp.inf, F32)
    acc_ref[...] = jnp.zeros_like(acc_ref)

    diag_a = p * tq
    diag_b = (nq - 1 - p) * tq

    def tile(n):
        if isinstance(n, int) and n < 2:
            return n, (diag_a, diag_b)[n]
        t = n - 2
        full_side = (t >= p).astype(jnp.int32)
        full_start = (t - p * full_side) * tq
        if isinstance(n, int):
            return full_side, full_start
        side = jnp.where(n < 2, n, full_side)
        return side, jnp.where(n == 0, diag_a, jnp.where(n == 1, diag_b, full_start))

    def scores(n, masked=False):
        side, start = tile(n)
        k = k_ref[pl.ds(pl.multiple_of(start, tq), tq), :]
        s = _dot_nt(qq_ref[side], k)
        if masked:
            qc = lax.broadcasted_iota(jnp.int32, (2 * tq, tq), 0)
            qc = jnp.where(qc >= tq, qc - tq, qc) // CHUNK
            kc = lax.broadcasted_iota(jnp.int32, (2 * tq, tq), 1) // CHUNK
            s = jnp.where(kc <= qc, s, NEG_INF)
        return s

    def process(buf, n):
        side, start = tile(n)
        s = s_ref[buf]
        m_old = m_ref[side]
        m_new = jnp.maximum(m_old, jnp.max(s, axis=1, keepdims=True))
        alpha = jnp.exp2(m_old - m_new)
        pr = jnp.exp2(s - jnp.concatenate([m_new] * (tq // HEAD), axis=1)).astype(BF16)
        pv = _dot(pr, vext_ref[pl.ds(pl.multiple_of(start, tq), tq), :])
        acc_ref[side] = jnp.concatenate([alpha, alpha], axis=1) * acc_ref[side] + pv
        m_ref[side] = m_new

    ntiles = nq + 1
    s_ref[0] = scores(0, masked=True)
    s_ref[1] = scores(1, masked=True)
    process(0, 0)
    s_ref[0] = scores(2)
    process(1, 1)

    def pair(u, carry):
        n = 2 * u
        s_ref[1] = scores(n + 1)
        process(0, n)
        s_ref[0] = scores(n + 2)
        process(1, n + 1)
        return carry

    lax.fori_loop(1, ntiles // 2, pair, 0)
    process(0, ntiles - 1)

    lp = lam_ref[...]
    lam = (jnp.exp(jnp.sum(lp[0:1] * lp[1:2], axis=1, keepdims=True))
           - jnp.exp(jnp.sum(lp[2:3] * lp[3:4], axis=1, keepdims=True)) + lambda_init)
    for side in range(2):
        acc = acc_ref[side]
        o1 = acc[:tq, :HEAD] / acc[:tq, HEAD:]
        o2 = acc[tq:, :HEAD] / acc[tq:, HEAD:]
        o = o1 - lam * o2
        ms = jnp.mean(o * o, axis=-1, keepdims=True)
        o = o * lax.rsqrt(ms + 1e-5) * sw_ref[...] * (1.0 - lambda_init)
        rows = slice(side * tq, (side + 1) * tq)
        o_ref[rows, :] = (o * _silu(z_ref[rows, :])).astype(o_ref.dtype)


def _diff_attn(q, k, v, z, lam_p, subln_w, lambda_init, tq):
    B, S, _ = q.shape
    nq = S // tq
    blk = pl.BlockSpec((None, 2 * tq, HEAD), lambda b, h, p: (b, p, h))
    kv = pl.BlockSpec((None, S, HEAD), lambda b, h, p: (b, 0, h))
    return pl.pallas_call(
        functools.partial(_attn_kernel, tq=tq, nq=nq, lambda_init=lambda_init),
        grid=(B, NHEAD, nq // 2),
        in_specs=[
            pl.BlockSpec((4, HEAD // 2), lambda b, h, p: (0, 0)),
            pl.BlockSpec((1, HEAD), lambda b, h, p: (0, 0)),
            blk, kv, kv, blk,
        ],
        out_specs=blk,
        out_shape=jax.ShapeDtypeStruct((B, S, DBR), BF16),
        scratch_shapes=[
            pltpu.VMEM((S, 2 * HEAD), BF16),
            pltpu.VMEM((2, 2 * tq, HEAD), BF16),
            pltpu.VMEM((2, 2 * tq, tq), F32),
            pltpu.VMEM((2, 2 * tq, HEAD), F32),
            pltpu.VMEM((2, 2 * tq, 2 * HEAD), F32),
        ],
        compiler_params=_cparams(("arbitrary", "arbitrary", "arbitrary")),
        name="diff_attn",
    )(lam_p, subln_w, q, k, v, z)


def _split_bf16(a, terms):
    out = []
    for _ in range(terms):
        piece = a.astype(BF16)
        out.append(piece)
        a = a - piece.astype(F32)
    return out


class _GdnGroup:
    def __init__(self, bi, tt, refs, consts):
        self.bi, self.tt = bi, tt
        self.qkv_ref, self.z_ref, self.ba_ref, self.alog_ref, self.dtb_ref, self.o_ref, self.state_ref = refs
        self.blk_tri, self.blk_strict, self.eye, self.tril01, self.nw = consts

    def prepare(self):
        bi, tt = self.bi, self.tt
        ba = self.ba_ref[bi]
        beta_all = _sigmoid(ba)
        g = -jnp.exp(self.alog_ref[...]) * _softplus(ba + self.dtb_ref[...])
        gcum = sum(_dot(self.tril01, piece) for piece in _split_bf16(g, 3))
        gcum_t = gcum.T
        self.qs, self.rhs, self.xs, self.nbs = [], [], [], []
        self.a_blks, self.kdts, self.decays = [], [], []
        self.o_raw = [[] for _ in range(NHEAD)]
        for h in range(NHEAD):
            qh = self.qkv_ref[bi, :, h * HEAD:(h + 1) * HEAD]
            kh = self.qkv_ref[bi, :, DBR + h * HEAD:DBR + (h + 1) * HEAD]
            vh = self.qkv_ref[bi, :, 2 * DBR + h * HEAD:2 * DBR + (h + 1) * HEAD]
            beta = jnp.broadcast_to(beta_all[:, h:h + 1], (tt, HEAD))
            gc = jnp.broadcast_to(gcum[:, NHEAD + h:NHEAD + h + 1], (tt, HEAD))
            eg = jnp.exp(gc)
            grow = gcum_t[NHEAD + h:NHEAD + h + 1, :]
            gcw = jnp.concatenate([gc] * (tt // HEAD), axis=1)
            decay = jnp.exp(jnp.where(self.blk_tri, gcw - grow, -jnp.inf))
            kb = kh * beta
            kq = _dot_nt(jnp.concatenate([qh, kb], axis=0).astype(BF16), kh.astype(BF16))
            a_bd = jnp.where(self.blk_tri, kq[:tt] * decay, 0.0).astype(BF16)
            lmat = jnp.where(self.blk_strict, kq[tt:] * decay, 0.0)
            self.rhs.append(jnp.concatenate([vh * beta, kb * eg], axis=1).astype(BF16))
            self.qs.append(qh * eg)
            blocks, kdts, decays = [], [], []
            for ch in range(tt // CHUNK):
                lo, hi = ch * CHUNK, (ch + 1) * CHUNK
                g_last = gc[hi - 1:hi]
                blocks.append(a_bd[lo:hi, lo:hi])
                kdts.append((kh[lo:hi] * jnp.exp(g_last - gc[lo:hi])).T.astype(BF16))
                decays.append(jnp.exp(g_last))
            self.a_blks.append(blocks)
            self.kdts.append(kdts)
            self.decays.append(decays)
            self.xs.append(self.eye - lmat)
            self.nbs.append((-lmat).astype(BF16))

    def inverse_level(self):
        self.nbs = [_dot(nb, nb).astype(BF16) for nb in self.nbs]
        self.xs = [x + _dot(x.astype(BF16), nb) for x, nb in zip(self.xs, self.nbs)]

    def finish_wy(self):
        self.uws = [_dot(x.astype(BF16), rh) for x, rh in zip(self.xs, self.rhs)]
        self.wqes = [jnp.concatenate([uw[:, HEAD:], qe], axis=1).astype(BF16)
                     for uw, qe in zip(self.uws, self.qs)]
        self.sts = [self.state_ref[self.bi * NHEAD + h] for h in range(NHEAD)]

    def chunk_read(self, ch):
        lo, hi = ch * CHUNK, (ch + 1) * CHUNK
        self.ws_qs = []
        for h in range(NHEAD):
            stb = self.sts[h].astype(BF16)
            self.ws_qs.append((_dot(self.wqes[h][lo:hi, :HEAD], stb), _dot(self.wqes[h][lo:hi, HEAD:], stb)))

    def chunk_update(self, ch):
        lo, hi = ch * CHUNK, (ch + 1) * CHUNK
        for h in range(NHEAD):
            w_s, q_s = self.ws_qs[h]
            v_new = (self.uws[h][lo:hi, :HEAD] - w_s).astype(BF16)
            self.o_raw[h].append(q_s + _dot(self.a_blks[h][ch], v_new))
            self.sts[h] = self.sts[h] * self.decays[h][ch] + _dot(self.kdts[h][ch], v_new)

    def finish_head(self, h):
        bi = self.bi
        o = jnp.concatenate(self.o_raw[h], axis=0)
        ms = jnp.mean(o * o, axis=-1, keepdims=True)
        o = o * lax.rsqrt(ms + 1e-6) * self.nw
        zz = self.z_ref[bi, :, h * HEAD:(h + 1) * HEAD]
        self.o_ref[bi, :, h * HEAD:(h + 1) * HEAD] = (o * _silu(zz)).astype(self.o_ref.dtype)
        self.state_ref[bi * NHEAD + h] = self.sts[h]

    def finish_stages(self):
        return [functools.partial(self.finish_head, h) for h in range(NHEAD)]


_INVERSE_LEVELS = 5


def _gdn_kernel(qkv_ref, z_ref, ba_ref, alog_ref, dtb_ref, nw_ref, o_ref, state_ref, *, tt, nb):
    t = pl.program_id(1)

    @pl.when(t == 0)
    def _():
        state_ref[...] = jnp.zeros_like(state_ref)

    r = lax.broadcasted_iota(jnp.int32, (tt, tt), 0)
    c = lax.broadcasted_iota(jnp.int32, (tt, tt), 1)
    same_chunk = (r // CHUNK) == (c // CHUNK)
    blk_tri = same_chunk & (c <= r)
    blk_strict = same_chunk & (c < r)
    consts = (blk_tri, blk_strict, (r == c).astype(F32), blk_tri.astype(BF16), nw_ref[...])
    refs = (qkv_ref, z_ref, ba_ref, alog_ref, dtb_ref, o_ref, state_ref)
    groups = [_GdnGroup(bi, tt, refs, consts) for bi in range(nb)]
    nchunk = tt // CHUNK

    def recurrence_with(group, fill):
        slots = [[] for _ in range(2 * nchunk)]
        for i, stage in enumerate(fill):
            slots[i * 2 * nchunk // len(fill)].append(stage)
        for ch in range(nchunk):
            group.chunk_read(ch)
            for stage in slots[2 * ch]:
                stage()
            group.chunk_update(ch)
            for stage in slots[2 * ch + 1]:
                stage()

    first = groups[0]
    first.prepare()
    for _ in range(_INVERSE_LEVELS):
        first.inverse_level()
    first.finish_wy()
    pending = []
    for prev, cur in zip(groups[:-1], groups[1:]):
        wy = [cur.prepare] + [cur.inverse_level] * _INVERSE_LEVELS + [cur.finish_wy]
        recurrence_with(prev, wy + pending)
        pending = prev.finish_stages()
    recurrence_with(groups[-1], pending)
    for stage in groups[-1].finish_stages():
        stage()


def _gdn(qkv, z, ba, alog_row, dtb_row, norm_w, tt, nb):
    B, S, _ = qkv.shape
    row = lambda w: pl.BlockSpec((1, w), lambda b, t: (0, 0))
    return pl.pallas_call(
        functools.partial(_gdn_kernel, tt=tt, nb=nb),
        grid=(B // nb, S // tt),
        in_specs=[
            pl.BlockSpec((nb, tt, 3 * DBR), lambda b, t: (b, t, 0)),
            pl.BlockSpec((nb, tt, DBR), lambda b, t: (b, t, 0)),
            pl.BlockSpec((nb, tt, HEAD), lambda b, t: (b, t, 0)),
            row(HEAD), row(HEAD), row(HEAD),
        ],
        out_specs=pl.BlockSpec((nb, tt, DBR), lambda b, t: (b, t, 0)),
        out_shape=jax.ShapeDtypeStruct((B, S, DBR), BF16),
        scratch_shapes=[pltpu.VMEM((nb * NHEAD, HEAD, HEAD), F32)],
        compiler_params=_cparams(("arbitrary", "arbitrary")),
        name="gdn",
    )(qkv, z, ba, alog_row, dtb_row, norm_w)


def _lru_kernel(x_ref, z_ref, gw_ref, gb_ref, lam_ref, o_ref, hprev_ref, *, tt):
    t = pl.program_id(1)

    @pl.when(t == 0)
    def _():
        hprev_ref[...] = jnp.zeros_like(hprev_ref)

    xc = x_ref[...]

    gates = [_dot(xc[:, n * HEAD:(n + 1) * HEAD].astype(BF16), gw_ref[n]) for n in range(NHEAD)]
    gi = jnp.concatenate([gt[:, :HEAD] for gt in gates], axis=1) + gb_ref[0:1, :]
    gr = jnp.concatenate([gt[:, HEAD:] for gt in gates], axis=1) + gb_ref[1:2, :]
    i_t = _sigmoid(gi)
    r_t = _sigmoid(gr)
    log_a = -LRU_C * r_t * _softplus(-lam_ref[...])
    a = jnp.exp(log_a)
    one_m_a2 = -jnp.tanh(log_a) * (a * a + 1.0)
    root = jnp.where(one_m_a2 > 0.0, one_m_a2 * lax.rsqrt(one_m_a2), 0.0)
    b = root * (i_t * xc)

    sub = lax.broadcasted_iota(jnp.int32, (tt, DBR), 0) % 8
    d = 1
    while d < 8:
        keep = sub >= d
        a_sh = pltpu.roll(a, d, 0)
        b_sh = pltpu.roll(b, d, 0)
        b = jnp.where(keep, a * b_sh + b, b)
        a = jnp.where(keep, a * a_sh, a)
        d *= 2
    carry = hprev_ref[...]
    groups = []
    for k in range(tt // 8):
        hk = b[8 * k:8 * k + 8] + a[8 * k:8 * k + 8] * carry
        groups.append(hk)
        carry = hk[7:8]
    h = jnp.concatenate(groups, axis=0)
    hprev_ref[...] = carry
    o_ref[...] = (h * _silu(z_ref[...])).astype(o_ref.dtype)


def _rglru(xc, z, gate_w, gate_b, lam, tt):
    B, S, _ = xc.shape
    blk = pl.BlockSpec((None, tt, DBR), lambda b, t: (b, t, 0))
    return pl.pallas_call(
        functools.partial(_lru_kernel, tt=tt),
        grid=(B, S // tt),
        in_specs=[
            blk, blk,
            pl.BlockSpec((NHEAD, HEAD, 2 * HEAD), lambda b, t: (0, 0, 0)),
            pl.BlockSpec((2, DBR), lambda b, t: (0, 0)),
            pl.BlockSpec((1, DBR), lambda b, t: (0, 0)),
        ],
        out_specs=blk,
        out_shape=jax.ShapeDtypeStruct((B, S, DBR), BF16),
        scratch_shapes=[pltpu.VMEM((1, DBR), F32)],
        compiler_params=_cparams(("arbitrary", "arbitrary")),
        name="rglru",
    )(xc, z, gate_w, gate_b, lam)


def _out_proj_kernel(ya_ref, yg_ref, yr_ref, w_ref, x_ref, fw_ref, o_ref, *, final_norm):
    y = x_ref[...]
    for n, y_ref in enumerate((ya_ref, yg_ref, yr_ref)):
        y = y + _dot(y_ref[...], w_ref[n * DBR:(n + 1) * DBR, :])
    if final_norm:
        ms = jnp.mean(y * y, axis=-1, keepdims=True)
        y = y * lax.rsqrt(ms + 1e-6) * fw_ref[...]
    o_ref[...] = y


def _out_proj(ya, yg, yr, w_out, x2d, fw, tm, tq, nq, final_norm):
    M, D = x2d.shape
    yblk = pl.BlockSpec((tm, DBR), lambda i: (i, 0))
    yablk = pl.BlockSpec((tm, DBR), lambda i: (_paired_rows(i, tm, tq, nq), 0))
    xblk = pl.BlockSpec((tm, D), lambda i: (i, 0))
    return pl.pallas_call(
        functools.partial(_out_proj_kernel, final_norm=final_norm),
        grid=(M // tm,),
        in_specs=[yablk, yblk, yblk,
                  pl.BlockSpec((3 * DBR, D), lambda i: (0, 0)),
                  xblk,
                  pl.BlockSpec((1, D), lambda i: (0, 0))],
        out_specs=xblk,
        out_shape=jax.ShapeDtypeStruct((M, D), F32),
        compiler_params=_cparams(("arbitrary",)),
        name="out_proj",
    )(ya, yg, yr, w_out, x2d, fw)


def _permute_w_in(w):
    d = DBR
    gba = jnp.pad(w[:, 8 * d:8 * d + 2 * NHEAD], ((0, 0), (0, HEAD - 2 * NHEAD)))
    return jnp.concatenate([w[:, :8 * d], w[:, 8 * d + 2 * NHEAD:], gba], axis=1).astype(BF16)


def _lane_row(vals, offset):
    return jnp.zeros((1, HEAD), F32).at[0, offset:offset + vals.shape[0]].set(vals.astype(F32))


def kernel(x, norm_w, w_in, attn_lambda, attn_subln_w, gdn_conv_w, gdn_a_log, gdn_dt_bias,
           gdn_norm_w, lru_conv_w, lru_conv_b, lru_gate_w, lru_gate_b, lru_log_param,
           w_out, final_norm_w):
    B, S, D = x.shape
    depth = w_in.shape[0]
    M = B * S
    tm = min(512, M)
    tq = min(512, S // 2)
    nq = S // tq
    tt = min(256, S)
    x2d = x.reshape(M, D).astype(F32)
    fw = final_norm_w.reshape(1, D).astype(F32)
    for l in range(depth):
        wp = _permute_w_in(w_in[l])
        conv_w = jnp.concatenate([gdn_conv_w[l], lru_conv_w[l]], axis=1).astype(F32)
        aq, ak, av, az, gqkv, gz, rxc, rz, gba = _in_proj(
            x2d, norm_w[l].reshape(1, D), wp, conv_w, lru_conv_b[l].reshape(1, DBR).astype(F32), tm, tq, nq)
        r3 = lambda a: a.reshape(B, S, a.shape[-1])
        lambda_init = 0.8 - 0.6 * math.exp(-0.3 * l)
        ya = _diff_attn(r3(aq), r3(ak), r3(av), r3(az), attn_lambda[l].astype(F32),
                        attn_subln_w[l].reshape(1, HEAD).astype(F32), lambda_init, tq)
        yg = _gdn(r3(gqkv), r3(gz), r3(gba),
                  _lane_row(gdn_a_log[l], NHEAD), _lane_row(gdn_dt_bias[l], NHEAD),
                  gdn_norm_w[l].reshape(1, HEAD).astype(F32), tt, 2 if B % 2 == 0 else 1)
        gw = jnp.concatenate([lru_gate_w[l, 0], lru_gate_w[l, 1]], axis=-1).astype(BF16)
        yr = _rglru(r3(rxc), r3(rz), gw, lru_gate_b[l].astype(F32),
                    lru_log_param[l].reshape(1, DBR).astype(F32), tt)
        x2d = _out_proj(ya.reshape(M, DBR), yg.reshape(M, DBR), yr.reshape(M, DBR),
                        w_out[l].astype(BF16), x2d, fw, tm, tq, nq, final_norm=(l == depth - 1))
    return x2d.reshape(B, S, D).astype(x.dtype)
```

```python
import functools
import math

import jax
import jax.numpy as jnp
from jax import lax
from jax.experimental import pallas as pl
from jax.experimental.pallas import tpu as pltpu

F32 = jnp.float32
BF16 = jnp.bfloat16

CHUNK = 64
HEAD = 128
NHEAD = 4
DBR = NHEAD * HEAD
CONV_K = 4
LRU_C = 8.0
NEG_INF = -1e30
LOG2E = 1.4426950408889634
VMEM_LIMIT = 56 * 1024 * 1024

_G_AQ, _G_AK, _G_AV, _G_AZ, _G_GQKV, _G_GZ, _G_RX, _G_RZ, _G_GBA = range(9)
_GROUP_WIDTH = (DBR, DBR, DBR, DBR, 3 * DBR, DBR, DBR, DBR, HEAD)
_GROUP_DTYPE = (BF16, BF16, BF16, F32, F32, F32, F32, F32, F32)
IN_COLS_PAD = sum(_GROUP_WIDTH)
CONV_COLS = 4 * DBR
_ITEM_COLS = 2 * HEAD


def _cparams(sem):
    return pltpu.CompilerParams(dimension_semantics=sem, vmem_limit_bytes=VMEM_LIMIT)


def _sigmoid(x):
    return 1.0 / (1.0 + jnp.exp(-x))


def _silu(x):
    return x * _sigmoid(x)


def _softplus(x):
    return jnp.maximum(x, 0.0) + jnp.log1p(jnp.exp(-jnp.abs(x)))


def _dot(a, b):
    return jnp.dot(a, b, preferred_element_type=F32)


def _dot_nt(a, b):
    return lax.dot_general(a, b, (((1,), (1,)), ((), ())), preferred_element_type=F32)


def _dot_tn(a, b):
    return lax.dot_general(a, b, (((0,), (0,)), ((), ())), preferred_element_type=F32)


def _in_proj_kernel(x_ref, nw_ref, w_ref, cw_ref, cb_ref, *refs, tm, tiles_per_seq):
    out_refs, xpad_ref = refs[:-1], refs[-1]
    i = pl.program_id(0)

    @pl.when(i % tiles_per_seq == 0)
    def _():
        xpad_ref[0:8, :] = jnp.zeros((8, CONV_COLS), F32)

    @pl.when(i % tiles_per_seq != 0)
    def _():
        xpad_ref[0:8, :] = xpad_ref[tm:tm + 8, :]

    xf = x_ref[...]
    ms = jnp.mean(xf * xf, axis=-1, keepdims=True)
    h = (xf * lax.rsqrt(ms + 1e-6) * nw_ref[...]).astype(BF16)
    plain, conv = [], []
    c0 = cc = 0
    for g, wdt in enumerate(_GROUP_WIDTH):
        for s in range(0, wdt, _ITEM_COLS):
            if g in (_G_GQKV, _G_RX):
                conv.append((g, s, c0 + s, cc))
                cc += _ITEM_COLS
            else:
                plain.append((g, s, c0 + s, None))
        c0 += wdt
    items = []
    while plain or conv:
        items += plain[:1] + conv[:1]
        plain, conv = plain[1:], conv[1:]

    for g, s, wc, cc in items:
        w = min(_ITEM_COLS, _GROUP_WIDTH[g] - s)
        r = _dot(h, w_ref[:, wc:wc + w])
        if g == _G_AQ:
            r = r * (float(HEAD // 2) ** -0.5 * LOG2E)
        if cc is not None:
            cols = slice(cc, cc + w)
            xpad_ref[8:, cols] = r
            cw = cw_ref[:, cols]
            r = r * cw[3:4]
            for j in range(CONV_K - 1):
                r = r + xpad_ref[5 + j:5 + j + tm, cols] * cw[j:j + 1]
            r = _silu(r) if g == _G_GQKV else r + cb_ref[:, s:s + w]
            if g == _G_GQKV and s < 2 * DBR:
                scale = HEAD ** -0.5 if s < DBR else 1.0
                heads = [r[:, c:c + HEAD] for c in range(0, w, HEAD)]
                heads = [x * (lax.rsqrt(jnp.sum(x * x, axis=-1, keepdims=True) + 1e-6) * scale)
                         for x in heads]
                r = jnp.concatenate(heads, axis=1)
        out_refs[g][:, s:s + w] = r.astype(out_refs[g].dtype)


def _paired_tile(j, nq):
    return jnp.where(j < nq // 2, 2 * j, 2 * (nq - 1 - j) + 1)


def _paired_rows(i, tm, tq, nq):
    per_q = tq // tm
    per_seq = nq * per_q
    b, within = i // per_seq, i % per_seq
    return b * per_seq + _paired_tile(within // per_q, nq) * per_q + within % per_q


def _in_proj(x2d, nw, w_perm, conv_w, conv_b, tm, tq, nq):
    M, D = x2d.shape
    out_shape = [jax.ShapeDtypeStruct((M, w), dt) for w, dt in zip(_GROUP_WIDTH, _GROUP_DTYPE)]
    out_specs = [pl.BlockSpec((tm, w), lambda i: (i, 0)) for w in _GROUP_WIDTH]
    for g in (_G_AQ, _G_AZ):
        out_specs[g] = pl.BlockSpec((tm, _GROUP_WIDTH[g]), lambda i: (_paired_rows(i, tm, tq, nq), 0))
    return pl.pallas_call(
        functools.partial(_in_proj_kernel, tm=tm, tiles_per_seq=nq * tq // tm),
        grid=(M // tm,),
        in_specs=[
            pl.BlockSpec((tm, D), lambda i: (i, 0)),
            pl.BlockSpec((1, D), lambda i: (0, 0)),
            pl.BlockSpec((D, IN_COLS_PAD), lambda i: (0, 0)),
            pl.BlockSpec((CONV_K, CONV_COLS), lambda i: (0, 0)),
            pl.BlockSpec((1, DBR), lambda i: (0, 0)),
        ],
        out_specs=out_specs,
        out_shape=out_shape,
        scratch_shapes=[pltpu.VMEM((tm + 8, CONV_COLS), F32)],
        compiler_params=_cparams(("arbitrary",)),
        name="in_proj",
    )(x2d, nw, w_perm, conv_w, conv_b)


def _attn_kernel(lam_ref, sw_ref, q_ref, k_ref, v_ref, z_ref, o_ref,
                 vext_ref, qq_ref, s_ref, m_ref, acc_ref, *, tq, nq, lambda_init):
    p = pl.program_id(2)

    @pl.when(p == 0)
    def _():
        vext_ref[:, :HEAD] = v_ref[...]
        vext_ref[:, HEAD:] = jnp.ones((v_ref.shape[0], HEAD), BF16)

    lane = lax.broadcasted_iota(jnp.int32, (tq, HEAD), 1)
    for side in range(2):
        q = q_ref[side * tq:(side + 1) * tq, :]
        zero = jnp.zeros_like(q)
        qq_ref[side, :tq, :] = jnp.where(lane < HEAD // 2, q, zero)
        qq_ref[side, tq:, :] = jnp.where(lane >= HEAD // 2, q, zero)
    m_ref[...] = jnp.full(m_ref.shape, -jnp.inf, F32)
    acc_ref[...] = jnp.zeros_like(acc_ref)

    diag_a = p * tq
    diag_b = (nq - 1 - p) * tq

    def tile(n):
        if isinstance(n, int) and n < 2:
            return n, (diag_a, diag_b)[n]
        t = n - 2
        full_side = (t >= p).astype(jnp.int32)
        full_start = (t - p * full_side) * tq
        if isinstance(n, int):
            return full_side, full_start
        side = jnp.where(n < 2, n, full_side)
        return side, jnp.where(n == 0, diag_a, jnp.where(n == 1, diag_b, full_start))

    def scores(n, masked=False):
        side, start = tile(n)
        k = k_ref[pl.ds(pl.multiple_of(start, tq), tq), :]
        s = _dot_nt(qq_ref[side], k)
        if masked:
            qc = lax.broadcasted_iota(jnp.int32, (2 * tq, tq), 0)
            qc = jnp.where(qc >= tq, qc - tq, qc) // CHUNK
            kc = lax.broadcasted_iota(jnp.int32, (2 * tq, tq), 1) // CHUNK
            s = jnp.where(kc <= qc, s, NEG_INF)
        return s

    def process(buf, n):
        side, start = tile(n)
        s = s_ref[buf]
        m_old = m_ref[side]
        m_new = jnp.maximum(m_old, jnp.max(s, axis=1, keepdims=True))
        alpha = jnp.exp2(m_old - m_new)
        pr = jnp.exp2(s - jnp.concatenate([m_new] * (tq // HEAD), axis=1)).astype(BF16)
        pv = _dot(pr, vext_ref[pl.ds(pl.multiple_of(start, tq), tq), :])
        acc_ref[side] = jnp.concatenate([alpha, alpha], axis=1) * acc_ref[side] + pv
        m_ref[side] = m_new

    ntiles = nq + 1
    s_ref[0] = scores(0, masked=True)
    s_ref[1] = scores(1, masked=True)
    process(0, 0)
    s_ref[0] = scores(2)
    process(1, 1)

    def pair(u, carry):
        n = 2 * u
        s_ref[1] = scores(n + 1)
        process(0, n)
        s_ref[0] = scores(n + 2)
        process(1, n + 1)
        return carry

    npairs = ntiles // 2 - 1

    group = 3

    def pair_group(w, carry):
        for i in range(group):
            pair(group * w + 1 + i, carry)
        return carry

    lax.fori_loop(0, npairs // group, pair_group, 0)
    lax.fori_loop(npairs - npairs % group + 1, npairs + 1 + jnp.minimum(p, 0), pair, 0)
    process(0, ntiles - 1)

    lp = lam_ref[...]
    lam = (jnp.exp(jnp.sum(lp[0:1] * lp[1:2], axis=1, keepdims=True))
           - jnp.exp(jnp.sum(lp[2:3] * lp[3:4], axis=1, keepdims=True)) + lambda_init)
    for side in range(2):
        acc = acc_ref[side]
        o1 = acc[:tq, :HEAD] / acc[:tq, HEAD:]
        o2 = acc[tq:, :HEAD] / acc[tq:, HEAD:]
        o = o1 - lam * o2
        ms = jnp.mean(o * o, axis=-1, keepdims=True)
        o = o * lax.rsqrt(ms + 1e-5) * sw_ref[...] * (1.0 - lambda_init)
        rows = slice(side * tq, (side + 1) * tq)
        o_ref[rows, :] = (o * _silu(z_ref[rows, :])).astype(o_ref.dtype)


def _diff_attn(q, k, v, z, lam_p, subln_w, lambda_init, tq):
    B, S, _ = q.shape
    nq = S // tq
    blk = pl.BlockSpec((None, 2 * tq, HEAD), lambda b, h, p: (b, p, h))
    kv = pl.BlockSpec((None, S, HEAD), lambda b, h, p: (b, 0, h))
    return pl.pallas_call(
        functools.partial(_attn_kernel, tq=tq, nq=nq, lambda_init=lambda_init),
        grid=(B, NHEAD, nq // 2),
        in_specs=[
            pl.BlockSpec((4, HEAD // 2), lambda b, h, p: (0, 0)),
            pl.BlockSpec((1, HEAD), lambda b, h, p: (0, 0)),
            blk, kv, kv, blk,
        ],
        out_specs=blk,
        out_shape=jax.ShapeDtypeStruct((B, S, DBR), BF16),
        scratch_shapes=[
            pltpu.VMEM((S, 2 * HEAD), BF16),
            pltpu.VMEM((2, 2 * tq, HEAD), BF16),
            pltpu.VMEM((2, 2 * tq, tq), F32),
            pltpu.VMEM((2, 2 * tq, HEAD), F32),
            pltpu.VMEM((2, 2 * tq, 2 * HEAD), F32),
        ],
        compiler_params=_cparams(("arbitrary", "arbitrary", "arbitrary")),
        name="diff_attn",
    )(lam_p, subln_w, q, k, v, z)


def _split_bf16(a, terms):
    out = []
    for _ in range(terms):
        piece = a.astype(BF16)
        out.append(piece)
        a = a - piece.astype(F32)
    return out


class _GdnGroup:
    def __init__(self, bi, tt, refs, consts):
        self.bi, self.tt = bi, tt
        self.qkv_ref, self.z_ref, self.ba_ref, self.alog_ref, self.dtb_ref, self.o_ref, self.state_ref = refs
        self.blk_tri, self.blk_strict, self.eye, self.tril01, self.nw = consts

    def prepare(self):
        bi, tt = self.bi, self.tt
        ba = self.ba_ref[bi]
        beta_all = _sigmoid(ba)
        g = -jnp.exp(self.alog_ref[...]) * _softplus(ba + self.dtb_ref[...])
        gcum = sum(_dot(self.tril01, piece) for piece in _split_bf16(g, 3))
        gcum_t = gcum.T
        self.qs, self.rhs, self.xs, self.nbs = [], [], [], []
        self.a_blks, self.kdts, self.decays = [], [], []
        self.o_raw = [[] for _ in range(NHEAD)]
        for h in range(NHEAD):
            qh = self.qkv_ref[bi, :, h * HEAD:(h + 1) * HEAD]
            kh = self.qkv_ref[bi, :, DBR + h * HEAD:DBR + (h + 1) * HEAD]
            vh = self.qkv_ref[bi, :, 2 * DBR + h * HEAD:2 * DBR + (h + 1) * HEAD]
            beta = jnp.broadcast_to(beta_all[:, h:h + 1], (tt, HEAD))
            gc = jnp.broadcast_to(gcum[:, NHEAD + h:NHEAD + h + 1], (tt, HEAD))
            eg = jnp.exp(gc)
            grow = gcum_t[NHEAD + h:NHEAD + h + 1, :]
            gcw = jnp.concatenate([gc] * (tt // HEAD), axis=1)
            decay = jnp.exp(jnp.where(self.blk_tri, gcw - grow, -jnp.inf))
            kb = kh * beta
            kq = _dot_nt(jnp.concatenate([qh, kb], axis=0).astype(BF16), kh.astype(BF16))
            a_bd = jnp.where(self.blk_tri, kq[:tt] * decay, 0.0).astype(BF16)
            lmat = jnp.where(self.blk_strict, kq[tt:] * decay, 0.0)
            self.rhs.append(jnp.concatenate([vh * beta, kb * eg], axis=1).astype(BF16))
            self.qs.append(qh * eg)
            blocks, kdts, decays = [], [], []
            for ch in range(tt // CHUNK):
                lo, hi = ch * CHUNK, (ch + 1) * CHUNK
                g_last = gc[hi - 1:hi]
                blocks.append(a_bd[lo:hi, lo:hi])
                kdts.append((kh[lo:hi] * jnp.exp(g_last - gc[lo:hi])).T.astype(BF16))
                decays.append(jnp.exp(g_last))
            self.a_blks.append(blocks)
            self.kdts.append(kdts)
            self.decays.append(decays)
            self.xs.append(self.eye - lmat)
            self.nbs.append((-lmat).astype(BF16))

    def inverse_level(self):
        self.nbs = [_dot(nb, nb).astype(BF16) for nb in self.nbs]
        self.xs = [x + _dot(x.astype(BF16), nb) for x, nb in zip(self.xs, self.nbs)]

    def finish_wy(self):
        self.uws = [_dot(x.astype(BF16), rh) for x, rh in zip(self.xs, self.rhs)]
        self.wqes = [jnp.concatenate([uw[:, HEAD:], qe], axis=1).astype(BF16)
                     for uw, qe in zip(self.uws, self.qs)]
        self.sts = [self.state_ref[self.bi * NHEAD + h] for h in range(NHEAD)]

    def chunk_read(self, ch):
        lo, hi = ch * CHUNK, (ch + 1) * CHUNK
        self.ws_qs = []
        for h in range(NHEAD):
            stb = self.sts[h].astype(BF16)
            self.ws_qs.append((_dot(self.wqes[h][lo:hi, :HEAD], stb), _dot(self.wqes[h][lo:hi, HEAD:], stb)))

    def chunk_update(self, ch):
        lo, hi = ch * CHUNK, (ch + 1) * CHUNK
        for h in range(NHEAD):
            w_s, q_s = self.ws_qs[h]
            v_new = (self.uws[h][lo:hi, :HEAD] - w_s).astype(BF16)
            self.o_raw[h].append(q_s + _dot(self.a_blks[h][ch], v_new))
            self.sts[h] = self.sts[h] * self.decays[h][ch] + _dot(self.kdts[h][ch], v_new)

    def finish_head(self, h):
        bi = self.bi
        o = jnp.concatenate(self.o_raw[h], axis=0)
        ms = jnp.mean(o * o, axis=-1, keepdims=True)
        o = o * lax.rsqrt(ms + 1e-6) * self.nw
        zz = self.z_ref[bi, :, h * HEAD:(h + 1) * HEAD]
        self.o_ref[bi, :, h * HEAD:(h + 1) * HEAD] = (o * _silu(zz)).astype(self.o_ref.dtype)
        self.state_ref[bi * NHEAD + h] = self.sts[h]

    def finish_stages(self):
        return [functools.partial(self.finish_head, h) for h in range(NHEAD)]


_INVERSE_LEVELS = 5


def _gdn_kernel(qkv_ref, z_ref, ba_ref, alog_ref, dtb_ref, nw_ref, o_ref, state_ref, *, tt, nb):
    t = pl.program_id(1)

    @pl.when(t == 0)
    def _():
        state_ref[...] = jnp.zeros_like(state_ref)

    r = lax.broadcasted_iota(jnp.int32, (tt, tt), 0)
    c = lax.broadcasted_iota(jnp.int32, (tt, tt), 1)
    same_chunk = (r // CHUNK) == (c // CHUNK)
    blk_tri = same_chunk & (c <= r)
    blk_strict = same_chunk & (c < r)
    consts = (blk_tri, blk_strict, (r == c).astype(F32), blk_tri.astype(BF16), nw_ref[...])
    refs = (qkv_ref, z_ref, ba_ref, alog_ref, dtb_ref, o_ref, state_ref)
    groups = [_GdnGroup(bi, tt, refs, consts) for bi in range(nb)]
    nchunk = tt // CHUNK

    def recurrence_with(group, fill):
        slots = [[] for _ in range(2 * nchunk)]
        for i, stage in enumerate(fill):
            slots[i * 2 * nchunk // len(fill)].append(stage)
        for ch in range(nchunk):
            group.chunk_read(ch)
            for stage in slots[2 * ch]:
                stage()
            group.chunk_update(ch)
            for stage in slots[2 * ch + 1]:
                stage()

    first = groups[0]
    first.prepare()
    for _ in range(_INVERSE_LEVELS):
        first.inverse_level()
    first.finish_wy()
    pending = []
    for prev, cur in zip(groups[:-1], groups[1:]):
        wy = [cur.prepare] + [cur.inverse_level] * _INVERSE_LEVELS + [cur.finish_wy]
        recurrence_with(prev, wy + pending)
        pending = prev.finish_stages()
    recurrence_with(groups[-1], pending)
    for stage in groups[-1].finish_stages():
        stage()


def _gdn(qkv, z, ba, alog_row, dtb_row, norm_w, tt, nb):
    B, S, _ = qkv.shape
    row = lambda w: pl.BlockSpec((1, w), lambda b, t: (0, 0))
    return pl.pallas_call(
        functools.partial(_gdn_kernel, tt=tt, nb=nb),
        grid=(B // nb, S // tt),
        in_specs=[
            pl.BlockSpec((nb, tt, 3 * DBR), lambda b, t: (b, t, 0)),
            pl.BlockSpec((nb, tt, DBR), lambda b, t: (b, t, 0)),
            pl.BlockSpec((nb, tt, HEAD), lambda b, t: (b, t, 0)),
            row(HEAD), row(HEAD), row(HEAD),
        ],
        out_specs=pl.BlockSpec((nb, tt, DBR), lambda b, t: (b, t, 0)),
        out_shape=jax.ShapeDtypeStruct((B, S, DBR), BF16),
        scratch_shapes=[pltpu.VMEM((nb * NHEAD, HEAD, HEAD), F32)],
        compiler_params=_cparams(("arbitrary", "arbitrary")),
        name="gdn",
    )(qkv, z, ba, alog_row, dtb_row, norm_w)


def _lru_kernel(x_ref, z_ref, gw_ref, gb_ref, lam_ref, o_ref, hprev_ref, *, tt):
    t = pl.program_id(1)

    @pl.when(t == 0)
    def _():
        hprev_ref[...] = jnp.zeros_like(hprev_ref)

    xc = x_ref[...]

    gates = [_dot(xc[:, n * HEAD:(n + 1) * HEAD].astype(BF16), gw_ref[n]) for n in range(NHEAD)]
    gi = jnp.concatenate([gt[:, :HEAD] for gt in gates], axis=1) + gb_ref[0:1, :]
    gr = jnp.concatenate([gt[:, HEAD:] for gt in gates], axis=1) + gb_ref[1:2, :]
    i_t = _sigmoid(gi)
    r_t = _sigmoid(gr)
    log_a = -LRU_C * r_t * _softplus(-lam_ref[...])
    a = jnp.exp(log_a)
    one_m_a2 = -jnp.tanh(log_a) * (a * a + 1.0)
    root = jnp.where(one_m_a2 > 0.0, one_m_a2 * lax.rsqrt(one_m_a2), 0.0)
    b = root * (i_t * xc)

    sub = lax.broadcasted_iota(jnp.int32, (tt, DBR), 0) % 8
    d = 1
    while d < 8:
        keep = sub >= d
        a_sh = pltpu.roll(a, d, 0)
        b_sh = pltpu.roll(b, d, 0)
        b = jnp.where(keep, a * b_sh + b, b)
        a = jnp.where(keep, a * a_sh, a)
        d *= 2
    carry = hprev_ref[...]
    groups = []
    for k in range(tt // 8):
        hk = b[8 * k:8 * k + 8] + a[8 * k:8 * k + 8] * carry
        groups.append(hk)
        carry = hk[7:8]
    h = jnp.concatenate(groups, axis=0)
    hprev_ref[...] = carry
    o_ref[...] = (h * _silu(z_ref[...])).astype(o_ref.dtype)


def _rglru(xc, z, gate_w, gate_b, lam, tt):
    B, S, _ = xc.shape
    blk = pl.BlockSpec((None, tt, DBR), lambda b, t: (b, t, 0))
    return pl.pallas_call(
        functools.partial(_lru_kernel, tt=tt),
        grid=(B, S // tt),
        in_specs=[
            blk, blk,
            pl.BlockSpec((NHEAD, HEAD, 2 * HEAD), lambda b, t: (0, 0, 0)),
            pl.BlockSpec((2, DBR), lambda b, t: (0, 0)),
            pl.BlockSpec((1, DBR), lambda b, t: (0, 0)),
        ],
        out_specs=blk,
        out_shape=jax.ShapeDtypeStruct((B, S, DBR), BF16),
        scratch_shapes=[pltpu.VMEM((1, DBR), F32)],
        compiler_params=_cparams(("arbitrary", "arbitrary")),
        name="rglru",
    )(xc, z, gate_w, gate_b, lam)


def _out_proj_kernel(ya_ref, yg_ref, yr_ref, w_ref, x_ref, fw_ref, o_ref, *, final_norm):
    y = x_ref[...]
    for n, y_ref in enumerate((ya_ref, yg_ref, yr_ref)):
        y = y + _dot(y_ref[...], w_ref[n * DBR:(n + 1) * DBR, :])
    if final_norm:
        ms = jnp.mean(y * y, axis=-1, keepdims=True)
        y = y * lax.rsqrt(ms + 1e-6) * fw_ref[...]
    o_ref[...] = y


def _out_proj(ya, yg, yr, w_out, x2d, fw, tm, tq, nq, final_norm):
    M, D = x2d.shape
    yblk = pl.BlockSpec((tm, DBR), lambda i: (i, 0))
    yablk = pl.BlockSpec((tm, DBR), lambda i: (_paired_rows(i, tm, tq, nq), 0))
    xblk = pl.BlockSpec((tm, D), lambda i: (i, 0))
    return pl.pallas_call(
        functools.partial(_out_proj_kernel, final_norm=final_norm),
        grid=(M // tm,),
        in_specs=[yablk, yblk, yblk,
                  pl.BlockSpec((3 * DBR, D), lambda i: (0, 0)),
                  xblk,
                  pl.BlockSpec((1, D), lambda i: (0, 0))],
        out_specs=xblk,
        out_shape=jax.ShapeDtypeStruct((M, D), F32),
        compiler_params=_cparams(("arbitrary",)),
        name="out_proj",
    )(ya, yg, yr, w_out, x2d, fw)


def _permute_w_in(w):
    d = DBR
    gba = jnp.pad(w[:, 8 * d:8 * d + 2 * NHEAD], ((0, 0), (0, HEAD - 2 * NHEAD)))
    return jnp.concatenate([w[:, :8 * d], w[:, 8 * d + 2 * NHEAD:], gba], axis=1).astype(BF16)


def _lane_row(vals, offset):
    return jnp.zeros((1, HEAD), F32).at[0, offset:offset + vals.shape[0]].set(vals.astype(F32))


def kernel(x, norm_w, w_in, attn_lambda, attn_subln_w, gdn_conv_w, gdn_a_log, gdn_dt_bias,
           gdn_norm_w, lru_conv_w, lru_conv_b, lru_gate_w, lru_gate_b, lru_log_param,
           w_out, final_norm_w):
    B, S, D = x.shape
    depth = w_in.shape[0]
    M = B * S
    tm = min(512, M)
    tq = min(512, S // 2)
    nq = S // tq
    tt = min(256, S)
    x2d = x.reshape(M, D).astype(F32)
    fw = final_norm_w.reshape(1, D).astype(F32)
    for l in range(depth):
        wp = _permute_w_in(w_in[l])
        conv_w = jnp.concatenate([gdn_conv_w[l], lru_conv_w[l]], axis=1).astype(F32)
        aq, ak, av, az, gqkv, gz, rxc, rz, gba = _in_proj(
            x2d, norm_w[l].reshape(1, D), wp, conv_w, lru_conv_b[l].reshape(1, DBR).astype(F32), tm, tq, nq)
        r3 = lambda a: a.reshape(B, S, a.shape[-1])
        lambda_init = 0.8 - 0.6 * math.exp(-0.3 * l)
        ya = _diff_attn(r3(aq), r3(ak), r3(av), r3(az), attn_lambda[l].astype(F32),
                        attn_subln_w[l].reshape(1, HEAD).astype(F32), lambda_init, tq)
        yg = _gdn(r3(gqkv), r3(gz), r3(gba),
                  _lane_row(gdn_a_log[l], NHEAD), _lane_row(gdn_dt_bias[l], NHEAD),
                  gdn_norm_w[l].reshape(1, HEAD).astype(F32), tt, 2 if B % 2 == 0 else 1)
        gw = jnp.concatenate([lru_gate_w[l, 0], lru_gate_w[l, 1]], axis=-1).astype(BF16)
        yr = _rglru(r3(rxc), r3(rz), gw, lru_gate_b[l].astype(F32),
                    lru_log_param[l].reshape(1, DBR).astype(F32), tt)
        x2d = _out_proj(ya.reshape(M, DBR), yg.reshape(M, DBR), yr.reshape(M, DBR),
                        w_out[l].astype(BF16), x2d, fw, tm, tq, nq, final_norm=(l == depth - 1))
    return x2d.reshape(B, S, D).astype(x.dtype)
```

```python
import functools
import math

import jax
import jax.numpy as jnp
from jax import lax
from jax.experimental import pallas as pl
from jax.experimental.pallas import tpu as pltpu

F32 = jnp.float32
BF16 = jnp.bfloat16

CHUNK = 64
HEAD = 128
NHEAD = 4
DBR = NHEAD * HEAD
CONV_K = 4
LRU_C = 8.0
NEG_INF = -1e30
LOG2E = 1.4426950408889634
VMEM_LIMIT = 56 * 1024 * 1024

_G_AQ, _G_AK, _G_AV, _G_AZ, _G_GQKV, _G_GZ, _G_RX, _G_RZ, _G_GBA = range(9)
_GROUP_WIDTH = (DBR, DBR, DBR, DBR, 3 * DBR, DBR, DBR, DBR, HEAD)
_GROUP_DTYPE = (BF16, BF16, BF16, F32, F32, F32, F32, F32, F32)
IN_COLS_PAD = sum(_GROUP_WIDTH)
CONV_COLS = 4 * DBR
_ITEM_COLS = 2 * HEAD


def _cparams(sem):
    return pltpu.CompilerParams(dimension_semantics=sem, vmem_limit_bytes=VMEM_LIMIT)


def _sigmoid(x):
    return 1.0 / (1.0 + jnp.exp(-x))


def _silu(x):
    return x * _sigmoid(x)


def _softplus(x):
    return jnp.maximum(x, 0.0) + jnp.log1p(jnp.exp(-jnp.abs(x)))


def _dot(a, b):
    return jnp.dot(a, b, preferred_element_type=F32)


def _dot_nt(a, b):
    return lax.dot_general(a, b, (((1,), (1,)), ((), ())), preferred_element_type=F32)


def _dot_tn(a, b):
    return lax.dot_general(a, b, (((0,), (0,)), ((), ())), preferred_element_type=F32)


def _in_proj_kernel(x_ref, nw_ref, w_ref, cw_ref, cb_ref, *refs, tm, tiles_per_seq):
    out_refs, xpad_ref = refs[:-1], refs[-1]
    i = pl.program_id(0)

    @pl.when(i % tiles_per_seq == 0)
    def _():
        xpad_ref[0:8, :] = jnp.zeros((8, CONV_COLS), F32)

    @pl.when(i % tiles_per_seq != 0)
    def _():
        xpad_ref[0:8, :] = xpad_ref[tm:tm + 8, :]

    xf = x_ref[...]
    ms = jnp.mean(xf * xf, axis=-1, keepdims=True)
    h = (xf * lax.rsqrt(ms + 1e-6) * nw_ref[...]).astype(BF16)
    plain, conv = [], []
    c0 = cc = 0
    for g, wdt in enumerate(_GROUP_WIDTH):
        for s in range(0, wdt, _ITEM_COLS):
            if g in (_G_GQKV, _G_RX):
                conv.append((g, s, c0 + s, cc))
                cc += _ITEM_COLS
            else:
                plain.append((g, s, c0 + s, None))
        c0 += wdt
    items = []
    while plain or conv:
        items += plain[:1] + conv[:1]
        plain, conv = plain[1:], conv[1:]

    for g, s, wc, cc in items:
        w = min(_ITEM_COLS, _GROUP_WIDTH[g] - s)
        r = _dot(h, w_ref[:, wc:wc + w])
        if g == _G_AQ:
            r = r * (float(HEAD // 2) ** -0.5 * LOG2E)
        if cc is not None:
            cols = slice(cc, cc + w)
            xpad_ref[8:, cols] = r
            cw = cw_ref[:, cols]
            r = r * cw[3:4]
            for j in range(CONV_K - 1):
                r = r + xpad_ref[5 + j:5 + j + tm, cols] * cw[j:j + 1]
            r = _silu(r) if g == _G_GQKV else r + cb_ref[:, s:s + w]
            if g == _G_GQKV and s < 2 * DBR:
                scale = HEAD ** -0.5 if s < DBR else 1.0
                heads = [r[:, c:c + HEAD] for c in range(0, w, HEAD)]
                heads = [x * (lax.rsqrt(jnp.sum(x * x, axis=-1, keepdims=True) + 1e-6) * scale)
                         for x in heads]
                r = jnp.concatenate(heads, axis=1)
        out_refs[g][:, s:s + w] = r.astype(out_refs[g].dtype)


def _paired_tile(j, nq):
    return jnp.where(j < nq // 2, 2 * j, 2 * (nq - 1 - j) + 1)


def _paired_rows(i, tm, tq, nq):
    per_q = tq // tm
    per_seq = nq * per_q
    b, within = i // per_seq, i % per_seq
    return b * per_seq + _paired_tile(within // per_q, nq) * per_q + within % per_q


def _in_proj(x2d, nw, w_perm, conv_w, conv_b, tm, tq, nq):
    M, D = x2d.shape
    out_shape = [jax.ShapeDtypeStruct((M, w), dt) for w, dt in zip(_GROUP_WIDTH, _GROUP_DTYPE)]
    out_specs = [pl.BlockSpec((tm, w), lambda i: (i, 0)) for w in _GROUP_WIDTH]
    for g in (_G_AQ, _G_AZ):
        out_specs[g] = pl.BlockSpec((tm, _GROUP_WIDTH[g]), lambda i: (_paired_rows(i, tm, tq, nq), 0))
    return pl.pallas_call(
        functools.partial(_in_proj_kernel, tm=tm, tiles_per_seq=nq * tq // tm),
        grid=(M // tm,),
        in_specs=[
            pl.BlockSpec((tm, D), lambda i: (i, 0)),
            pl.BlockSpec((1, D), lambda i: (0, 0)),
            pl.BlockSpec((D, IN_COLS_PAD), lambda i: (0, 0)),
            pl.BlockSpec((CONV_K, CONV_COLS), lambda i: (0, 0)),
            pl.BlockSpec((1, DBR), lambda i: (0, 0)),
        ],
        out_specs=out_specs,
        out_shape=out_shape,
        scratch_shapes=[pltpu.VMEM((tm + 8, CONV_COLS), F32)],
        compiler_params=_cparams(("arbitrary",)),
        name="in_proj",
    )(x2d, nw, w_perm, conv_w, conv_b)


def _attn_kernel(lam_ref, sw_ref, q_ref, k_ref, v_ref, z_ref, o_ref,
                 vext_ref, qq_ref, s_ref, m_ref, acc_ref, *, tq, nq, lambda_init):
    p = pl.program_id(2)

    @pl.when(p == 0)
    def _():
        vext_ref[:, :HEAD] = v_ref[...]
        vext_ref[:, HEAD:] = jnp.ones((v_ref.shape[0], HEAD), BF16)

    lane = lax.broadcasted_iota(jnp.int32, (tq, HEAD), 1)
    for side in range(2):
        q = q_ref[side * tq:(side + 1) * tq, :]
        zero = jnp.zeros_like(q)
        qq_ref[side, :tq, :] = jnp.where(lane < HEAD // 2, q, zero)
        qq_ref[side, tq:, :] = jnp.where(lane >= HEAD // 2, q, zero)
    m_ref[...] = jnp.full(m_ref.shape, -jnp.inf, F32)
    acc_ref[...] = jnp.zeros_like(acc_ref)

    diag_a = p * tq
    diag_b = (nq - 1 - p) * tq

    def tile(n):
        if isinstance(n, int) and n < 2:
            return n, (diag_a, diag_b)[n]
        t = n - 2
        full_side = (t >= p).astype(jnp.int32)
        full_start = (t - p * full_side) * tq
        if isinstance(n, int):
            return full_side, full_start
        side = jnp.where(n < 2, n, full_side)
        return side, jnp.where(n == 0, diag_a, jnp.where(n == 1, diag_b, full_start))

    def scores(n, masked=False):
        side, start = tile(n)
        k = k_ref[pl.ds(pl.multiple_of(start, tq), tq), :]
        s = _dot_nt(qq_ref[side], k)
        if masked:
            qc = lax.broadcasted_iota(jnp.int32, (2 * tq, tq), 0)
            qc = jnp.where(qc >= tq, qc - tq, qc) // CHUNK
            kc = lax.broadcasted_iota(jnp.int32, (2 * tq, tq), 1) // CHUNK
            s = jnp.where(kc <= qc, s, NEG_INF)
        return s

    def process(buf, n):
        side, start = tile(n)
        s = s_ref[buf]
        m_old = m_ref[side]
        m_new = jnp.maximum(m_old, jnp.max(s, axis=1, keepdims=True))
        alpha = jnp.exp2(m_old - m_new)
        pr = jnp.exp2(s - jnp.concatenate([m_new] * (tq // HEAD), axis=1)).astype(BF16)
        pv = _dot(pr, vext_ref[pl.ds(pl.multiple_of(start, tq), tq), :])
        acc_ref[side] = jnp.concatenate([alpha, alpha], axis=1) * acc_ref[side] + pv
        m_ref[side] = m_new

    ntiles = nq + 1
    s_ref[0] = scores(0, masked=True)
    s_ref[1] = scores(1, masked=True)
    process(0, 0)
    s_ref[0] = scores(2)
    process(1, 1)

    def pair(u, carry):
        n = 2 * u
        s_ref[1] = scores(n + 1)
        process(0, n)
        s_ref[0] = scores(n + 2)
        process(1, n + 1)
        return carry

    npairs = ntiles // 2 - 1

    group = 7

    def pair_group(w, carry):
        for i in range(group):
            pair(group * w + 1 + i, carry)
        return carry

    lax.fori_loop(0, npairs // group + jnp.minimum(p, 0), pair_group, 0)
    lax.fori_loop(npairs - npairs % group + 1, npairs + 1 + jnp.minimum(p, 0), pair, 0)
    process(0, ntiles - 1)

    lp = lam_ref[...]
    lam = (jnp.exp(jnp.sum(lp[0:1] * lp[1:2], axis=1, keepdims=True))
           - jnp.exp(jnp.sum(lp[2:3] * lp[3:4], axis=1, keepdims=True)) + lambda_init)
    for side in range(2):
        acc = acc_ref[side]
        o1 = acc[:tq, :HEAD] / acc[:tq, HEAD:]
        o2 = acc[tq:, :HEAD] / acc[tq:, HEAD:]
        o = o1 - lam * o2
        ms = jnp.mean(o * o, axis=-1, keepdims=True)
        o = o * lax.rsqrt(ms + 1e-5) * sw_ref[...] * (1.0 - lambda_init)
        rows = slice(side * tq, (side + 1) * tq)
        o_ref[rows, :] = (o * _silu(z_ref[rows, :])).astype(o_ref.dtype)


def _diff_attn(q, k, v, z, lam_p, subln_w, lambda_init, tq):
    B, S, _ = q.shape
    nq = S // tq
    blk = pl.BlockSpec((None, 2 * tq, HEAD), lambda b, h, p: (b, p, h))
    kv = pl.BlockSpec((None, S, HEAD), lambda b, h, p: (b, 0, h))
    return pl.pallas_call(
        functools.partial(_attn_kernel, tq=tq, nq=nq, lambda_init=lambda_init),
        grid=(B, NHEAD, nq // 2),
        in_specs=[
            pl.BlockSpec((4, HEAD // 2), lambda b, h, p: (0, 0)),
            pl.BlockSpec((1, HEAD), lambda b, h, p: (0, 0)),
            blk, kv, kv, blk,
        ],
        out_specs=blk,
        out_shape=jax.ShapeDtypeStruct((B, S, DBR), BF16),
        scratch_shapes=[
            pltpu.VMEM((S, 2 * HEAD), BF16),
            pltpu.VMEM((2, 2 * tq, HEAD), BF16),
            pltpu.VMEM((2, 2 * tq, tq), F32),
            pltpu.VMEM((2, 2 * tq, HEAD), F32),
            pltpu.VMEM((2, 2 * tq, 2 * HEAD), F32),
        ],
        compiler_params=_cparams(("arbitrary", "arbitrary", "arbitrary")),
        name="diff_attn",
    )(lam_p, subln_w, q, k, v, z)


def _split_bf16(a, terms):
    out = []
    for _ in range(terms):
        piece = a.astype(BF16)
        out.append(piece)
        a = a - piece.astype(F32)
    return out


class _GdnGroup:
    def __init__(self, bi, tt, refs, consts):
        self.bi, self.tt = bi, tt
        self.qkv_ref, self.z_ref, self.ba_ref, self.alog_ref, self.dtb_ref, self.o_ref, self.state_ref = refs
        self.blk_tri, self.blk_strict, self.eye, self.tril01, self.nw = consts

    def prepare(self):
        bi, tt = self.bi, self.tt
        ba = self.ba_ref[bi]
        beta_all = _sigmoid(ba)
        g = -jnp.exp(self.alog_ref[...]) * _softplus(ba + self.dtb_ref[...])
        gcum = sum(_dot(self.tril01, piece) for piece in _split_bf16(g, 3))
        gcum_t = gcum.T
        self.qs, self.rhs, self.xs, self.nbs = [], [], [], []
        self.a_blks, self.kdts, self.decays = [], [], []
        self.o_raw = [[] for _ in range(NHEAD)]
        for h in range(NHEAD):
            qh = self.qkv_ref[bi, :, h * HEAD:(h + 1) * HEAD]
            kh = self.qkv_ref[bi, :, DBR + h * HEAD:DBR + (h + 1) * HEAD]
            vh = self.qkv_ref[bi, :, 2 * DBR + h * HEAD:2 * DBR + (h + 1) * HEAD]
            beta = jnp.broadcast_to(beta_all[:, h:h + 1], (tt, HEAD))
            gc = jnp.broadcast_to(gcum[:, NHEAD + h:NHEAD + h + 1], (tt, HEAD))
            eg = jnp.exp(gc)
            grow = gcum_t[NHEAD + h:NHEAD + h + 1, :]
            gcw = jnp.concatenate([gc] * (tt // HEAD), axis=1)
            decay = jnp.exp(jnp.where(self.blk_tri, gcw - grow, -jnp.inf))
            kb = kh * beta
            kq = _dot_nt(jnp.concatenate([qh, kb], axis=0).astype(BF16), kh.astype(BF16))
            a_bd = jnp.where(self.blk_tri, kq[:tt] * decay, 0.0).astype(BF16)
            lmat = jnp.where(self.blk_strict, kq[tt:] * decay, 0.0)
            self.rhs.append(jnp.concatenate([vh * beta, kb * eg], axis=1).astype(BF16))
            self.qs.append(qh * eg)
            blocks, kdts, decays = [], [], []
            for ch in range(tt // CHUNK):
                lo, hi = ch * CHUNK, (ch + 1) * CHUNK
                g_last = gc[hi - 1:hi]
                blocks.append(a_bd[lo:hi, lo:hi])
                kdts.append((kh[lo:hi] * jnp.exp(g_last - gc[lo:hi])).T.astype(BF16))
                decays.append(jnp.exp(g_last))
            self.a_blks.append(blocks)
            self.kdts.append(kdts)
            self.decays.append(decays)
            self.xs.append(self.eye - lmat)
            self.nbs.append((-lmat).astype(BF16))

    def inverse_level(self):
        self.nbs = [_dot(nb, nb).astype(BF16) for nb in self.nbs]
        self.xs = [x + _dot(x.astype(BF16), nb) for x, nb in zip(self.xs, self.nbs)]

    def finish_wy(self):
        self.uws = [_dot(x.astype(BF16), rh) for x, rh in zip(self.xs, self.rhs)]
        self.wqes = [jnp.concatenate([uw[:, HEAD:], qe], axis=1).astype(BF16)
                     for uw, qe in zip(self.uws, self.qs)]
        self.sts = [self.state_ref[self.bi * NHEAD + h] for h in range(NHEAD)]

    def chunk_read(self, ch):
        lo, hi = ch * CHUNK, (ch + 1) * CHUNK
        self.ws_qs = []
        for h in range(NHEAD):
            stb = self.sts[h].astype(BF16)
            self.ws_qs.append((_dot(self.wqes[h][lo:hi, :HEAD], stb), _dot(self.wqes[h][lo:hi, HEAD:], stb)))

    def chunk_update(self, ch):
        lo, hi = ch * CHUNK, (ch + 1) * CHUNK
        for h in range(NHEAD):
            w_s, q_s = self.ws_qs[h]
            v_new = (self.uws[h][lo:hi, :HEAD] - w_s).astype(BF16)
            self.o_raw[h].append(q_s + _dot(self.a_blks[h][ch], v_new))
            self.sts[h] = self.sts[h] * self.decays[h][ch] + _dot(self.kdts[h][ch], v_new)

    def finish_head(self, h):
        bi = self.bi
        o = jnp.concatenate(self.o_raw[h], axis=0)
        ms = jnp.mean(o * o, axis=-1, keepdims=True)
        o = o * lax.rsqrt(ms + 1e-6) * self.nw
        zz = self.z_ref[bi, :, h * HEAD:(h + 1) * HEAD]
        self.o_ref[bi, :, h * HEAD:(h + 1) * HEAD] = (o * _silu(zz)).astype(self.o_ref.dtype)
        self.state_ref[bi * NHEAD + h] = self.sts[h]

    def finish_stages(self):
        return [functools.partial(self.finish_head, h) for h in range(NHEAD)]


_INVERSE_LEVELS = 5


def _gdn_kernel(qkv_ref, z_ref, ba_ref, alog_ref, dtb_ref, nw_ref, o_ref, state_ref, *, tt, nb):
    t = pl.program_id(1)

    @pl.when(t == 0)
    def _():
        state_ref[...] = jnp.zeros_like(state_ref)

    r = lax.broadcasted_iota(jnp.int32, (tt, tt), 0)
    c = lax.broadcasted_iota(jnp.int32, (tt, tt), 1)
    same_chunk = (r // CHUNK) == (c // CHUNK)
    blk_tri = same_chunk & (c <= r)
    blk_strict = same_chunk & (c < r)
    consts = (blk_tri, blk_strict, (r == c).astype(F32), blk_tri.astype(BF16), nw_ref[...])
    refs = (qkv_ref, z_ref, ba_ref, alog_ref, dtb_ref, o_ref, state_ref)
    groups = [_GdnGroup(bi, tt, refs, consts) for bi in range(nb)]
    nchunk = tt // CHUNK

    def recurrence_with(group, fill):
        slots = [[] for _ in range(2 * nchunk)]
        for i, stage in enumerate(fill):
            slots[i * 2 * nchunk // len(fill)].append(stage)
        for ch in range(nchunk):
            group.chunk_read(ch)
            for stage in slots[2 * ch]:
                stage()
            group.chunk_update(ch)
            for stage in slots[2 * ch + 1]:
                stage()

    first = groups[0]
    first.prepare()
    for _ in range(_INVERSE_LEVELS):
        first.inverse_level()
    first.finish_wy()
    pending = []
    for prev, cur in zip(groups[:-1], groups[1:]):
        wy = [cur.prepare] + [cur.inverse_level] * _INVERSE_LEVELS + [cur.finish_wy]
        recurrence_with(prev, wy + pending)
        pending = prev.finish_stages()
    recurrence_with(groups[-1], pending)
    for stage in groups[-1].finish_stages():
        stage()


def _gdn(qkv, z, ba, alog_row, dtb_row, norm_w, tt, nb):
    B, S, _ = qkv.shape
    row = lambda w: pl.BlockSpec((1, w), lambda b, t: (0, 0))
    return pl.pallas_call(
        functools.partial(_gdn_kernel, tt=tt, nb=nb),
        grid=(B // nb, S // tt),
        in_specs=[
            pl.BlockSpec((nb, tt, 3 * DBR), lambda b, t: (b, t, 0)),
            pl.BlockSpec((nb, tt, DBR), lambda b, t: (b, t, 0)),
            pl.BlockSpec((nb, tt, HEAD), lambda b, t: (b, t, 0)),
            row(HEAD), row(HEAD), row(HEAD),
        ],
        out_specs=pl.BlockSpec((nb, tt, DBR), lambda b, t: (b, t, 0)),
        out_shape=jax.ShapeDtypeStruct((B, S, DBR), BF16),
        scratch_shapes=[pltpu.VMEM((nb * NHEAD, HEAD, HEAD), F32)],
        compiler_params=_cparams(("arbitrary", "arbitrary")),
        name="gdn",
    )(qkv, z, ba, alog_row, dtb_row, norm_w)


def _lru_kernel(x_ref, z_ref, gw_ref, gb_ref, lam_ref, o_ref, hprev_ref, *, tt):
    t = pl.program_id(1)

    @pl.when(t == 0)
    def _():
        hprev_ref[...] = jnp.zeros_like(hprev_ref)

    xc = x_ref[...]

    gates = [_dot(xc[:, n * HEAD:(n + 1) * HEAD].astype(BF16), gw_ref[n]) for n in range(NHEAD)]
    gi = jnp.concatenate([gt[:, :HEAD] for gt in gates], axis=1) + gb_ref[0:1, :]
    gr = jnp.concatenate([gt[:, HEAD:] for gt in gates], axis=1) + gb_ref[1:2, :]
    i_t = _sigmoid(gi)
    r_t = _sigmoid(gr)
    log_a = -LRU_C * r_t * _softplus(-lam_ref[...])
    a = jnp.exp(log_a)
    one_m_a2 = -jnp.tanh(log_a) * (a * a + 1.0)
    root = jnp.where(one_m_a2 > 0.0, one_m_a2 * lax.rsqrt(one_m_a2), 0.0)
    b = root * (i_t * xc)

    sub = lax.broadcasted_iota(jnp.int32, (tt, DBR), 0) % 8
    d = 1
    while d < 8:
        keep = sub >= d
        a_sh = pltpu.roll(a, d, 0)
        b_sh = pltpu.roll(b, d, 0)
        b = jnp.where(keep, a * b_sh + b, b)
        a = jnp.where(keep, a * a_sh, a)
        d *= 2
    carry = hprev_ref[...]
    groups = []
    for k in range(tt // 8):
        hk = b[8 * k:8 * k + 8] + a[8 * k:8 * k + 8] * carry
        groups.append(hk)
        carry = hk[7:8]
    h = jnp.concatenate(groups, axis=0)
    hprev_ref[...] = carry
    o_ref[...] = (h * _silu(z_ref[...])).astype(o_ref.dtype)


def _rglru(xc, z, gate_w, gate_b, lam, tt):
    B, S, _ = xc.shape
    blk = pl.BlockSpec((None, tt, DBR), lambda b, t: (b, t, 0))
    return pl.pallas_call(
        functools.partial(_lru_kernel, tt=tt),
        grid=(B, S // tt),
        in_specs=[
            blk, blk,
            pl.BlockSpec((NHEAD, HEAD, 2 * HEAD), lambda b, t: (0, 0, 0)),
            pl.BlockSpec((2, DBR), lambda b, t: (0, 0)),
            pl.BlockSpec((1, DBR), lambda b, t: (0, 0)),
        ],
        out_specs=blk,
        out_shape=jax.ShapeDtypeStruct((B, S, DBR), BF16),
        scratch_shapes=[pltpu.VMEM((1, DBR), F32)],
        compiler_params=_cparams(("arbitrary", "arbitrary")),
        name="rglru",
    )(xc, z, gate_w, gate_b, lam)


def _out_proj_kernel(ya_ref, yg_ref, yr_ref, w_ref, x_ref, fw_ref, o_ref, *, final_norm):
    y = x_ref[...]
    for n, y_ref in enumerate((ya_ref, yg_ref, yr_ref)):
        y = y + _dot(y_ref[...], w_ref[n * DBR:(n + 1) * DBR, :])
    if final_norm:
        ms = jnp.mean(y * y, axis=-1, keepdims=True)
        y = y * lax.rsqrt(ms + 1e-6) * fw_ref[...]
    o_ref[...] = y


def _out_proj(ya, yg, yr, w_out, x2d, fw, tm, tq, nq, final_norm):
    M, D = x2d.shape
    yblk = pl.BlockSpec((tm, DBR), lambda i: (i, 0))
    yablk = pl.BlockSpec((tm, DBR), lambda i: (_paired_rows(i, tm, tq, nq), 0))
    xblk = pl.BlockSpec((tm, D), lambda i: (i, 0))
    return pl.pallas_call(
        functools.partial(_out_proj_kernel, final_norm=final_norm),
        grid=(M // tm,),
        in_specs=[yablk, yblk, yblk,
                  pl.BlockSpec((3 * DBR, D), lambda i: (0, 0)),
                  xblk,
                  pl.BlockSpec((1, D), lambda i: (0, 0))],
        out_specs=xblk,
        out_shape=jax.ShapeDtypeStruct((M, D), F32),
        compiler_params=_cparams(("arbitrary",)),
        name="out_proj",
    )(ya, yg, yr, w_out, x2d, fw)


def _permute_w_in(w):
    d = DBR
    gba = jnp.pad(w[:, 8 * d:8 * d + 2 * NHEAD], ((0, 0), (0, HEAD - 2 * NHEAD)))
    return jnp.concatenate([w[:, :8 * d], w[:, 8 * d + 2 * NHEAD:], gba], axis=1).astype(BF16)


def _lane_row(vals, offset):
    return jnp.zeros((1, HEAD), F32).at[0, offset:offset + vals.shape[0]].set(vals.astype(F32))


def kernel(x, norm_w, w_in, attn_lambda, attn_subln_w, gdn_conv_w, gdn_a_log, gdn_dt_bias,
           gdn_norm_w, lru_conv_w, lru_conv_b, lru_gate_w, lru_gate_b, lru_log_param,
           w_out, final_norm_w):
    B, S, D = x.shape
    depth = w_in.shape[0]
    M = B * S
    tm = min(512, M)
    tq = min(512, S // 2)
    nq = S // tq
    tt = min(256, S)
    x2d = x.reshape(M, D).astype(F32)
    fw = final_norm_w.reshape(1, D).astype(F32)
    for l in range(depth):
        wp = _permute_w_in(w_in[l])
        conv_w = jnp.concatenate([gdn_conv_w[l], lru_conv_w[l]], axis=1).astype(F32)
        aq, ak, av, az, gqkv, gz, rxc, rz, gba = _in_proj(
            x2d, norm_w[l].reshape(1, D), wp, conv_w, lru_conv_b[l].reshape(1, DBR).astype(F32), tm, tq, nq)
        r3 = lambda a: a.reshape(B, S, a.shape[-1])
        lambda_init = 0.8 - 0.6 * math.exp(-0.3 * l)
        ya = _diff_attn(r3(aq), r3(ak), r3(av), r3(az), attn_lambda[l].astype(F32),
                        attn_subln_w[l].reshape(1, HEAD).astype(F32), lambda_init, tq)
        yg = _gdn(r3(gqkv), r3(gz), r3(gba),
                  _lane_row(gdn_a_log[l], NHEAD), _lane_row(gdn_dt_bias[l], NHEAD),
                  gdn_norm_w[l].reshape(1, HEAD).astype(F32), tt, 2 if B % 2 == 0 else 1)
        gw = jnp.concatenate([lru_gate_w[l, 0], lru_gate_w[l, 1]], axis=-1).astype(BF16)
        yr = _rglru(r3(rxc), r3(rz), gw, lru_gate_b[l].astype(F32),
                    lru_log_param[l].reshape(1, DBR).astype(F32), tt)
        x2d = _out_proj(ya.reshape(M, DBR), yg.reshape(M, DBR), yr.reshape(M, DBR),
                        w_out[l].astype(BF16), x2d, fw, tm, tq, nq, final_norm=(l == depth - 1))
    return x2d.reshape(B, S, D).astype(x.dtype)
```

```python
import functools
import math

import jax
import jax.numpy as jnp
from jax import lax
from jax.experimental import pallas as pl
from jax.experimental.pallas import tpu as pltpu

F32 = jnp.float32
BF16 = jnp.bfloat16

CHUNK = 64
HEAD = 128
NHEAD = 4
DBR = NHEAD * HEAD
CONV_K = 4
LRU_C = 8.0
NEG_INF = -1e30
LOG2E = 1.4426950408889634
VMEM_LIMIT = 56 * 1024 * 1024

_G_AQ, _G_AK, _G_AV, _G_AZ, _G_GQKV, _G_GZ, _G_RX, _G_RZ, _G_GBA = range(9)
_GROUP_WIDTH = (DBR, DBR, DBR, DBR, 3 * DBR, DBR, DBR, DBR, HEAD)
_GROUP_DTYPE = (BF16, BF16, BF16, F32, F32, F32, F32, F32, F32)
IN_COLS_PAD = sum(_GROUP_WIDTH)
CONV_COLS = 4 * DBR
_ITEM_COLS = 2 * HEAD


def _cparams(sem):
    return pltpu.CompilerParams(dimension_semantics=sem, vmem_limit_bytes=VMEM_LIMIT)


def _sigmoid(x):
    return 1.0 / (1.0 + jnp.exp(-x))


def _silu(x):
    return x * _sigmoid(x)


def _softplus(x):
    return jnp.maximum(x, 0.0) + jnp.log1p(jnp.exp(-jnp.abs(x)))


def _dot(a, b):
    return jnp.dot(a, b, preferred_element_type=F32)


def _dot_nt(a, b):
    return lax.dot_general(a, b, (((1,), (1,)), ((), ())), preferred_element_type=F32)


def _dot_tn(a, b):
    return lax.dot_general(a, b, (((0,), (0,)), ((), ())), preferred_element_type=F32)


def _in_proj_kernel(x_ref, nw_ref, w_ref, cw_ref, cb_ref, *refs, tm, tiles_per_seq):
    out_refs, xpad_ref = refs[:-1], refs[-1]
    i = pl.program_id(0)

    @pl.when(i % tiles_per_seq == 0)
    def _():
        xpad_ref[0:8, :] = jnp.zeros((8, CONV_COLS), F32)

    @pl.when(i % tiles_per_seq != 0)
    def _():
        xpad_ref[0:8, :] = xpad_ref[tm:tm + 8, :]

    xf = x_ref[...]
    ms = jnp.mean(xf * xf, axis=-1, keepdims=True)
    h = (xf * lax.rsqrt(ms + 1e-6) * nw_ref[...]).astype(BF16)
    plain, conv = [], []
    c0 = cc = 0
    for g, wdt in enumerate(_GROUP_WIDTH):
        for s in range(0, wdt, _ITEM_COLS):
            if g in (_G_GQKV, _G_RX):
                conv.append((g, s, c0 + s, cc))
                cc += _ITEM_COLS
            else:
                plain.append((g, s, c0 + s, None))
        c0 += wdt
    items = []
    while plain or conv:
        items += plain[:1] + conv[:1]
        plain, conv = plain[1:], conv[1:]

    for g, s, wc, cc in items:
        w = min(_ITEM_COLS, _GROUP_WIDTH[g] - s)
        r = _dot(h, w_ref[:, wc:wc + w])
        if g == _G_AQ:
            r = r * (float(HEAD // 2) ** -0.5 * LOG2E)
        if cc is not None:
            cols = slice(cc, cc + w)
            xpad_ref[8:, cols] = r
            cw = cw_ref[:, cols]
            r = r * cw[3:4]
            for j in range(CONV_K - 1):
                r = r + xpad_ref[5 + j:5 + j + tm, cols] * cw[j:j + 1]
            r = _silu(r) if g == _G_GQKV else r + cb_ref[:, s:s + w]
            if g == _G_GQKV and s < 2 * DBR:
                scale = HEAD ** -0.5 if s < DBR else 1.0
                heads = [r[:, c:c + HEAD] for c in range(0, w, HEAD)]
                heads = [x * (lax.rsqrt(jnp.sum(x * x, axis=-1, keepdims=True) + 1e-6) * scale)
                         for x in heads]
                r = jnp.concatenate(heads, axis=1)
        out_refs[g][:, s:s + w] = r.astype(out_refs[g].dtype)


def _paired_tile(j, nq):
    return jnp.where(j < nq // 2, 2 * j, 2 * (nq - 1 - j) + 1)


def _paired_rows(i, tm, tq, nq):
    per_q = tq // tm
    per_seq = nq * per_q
    b, within = i // per_seq, i % per_seq
    return b * per_seq + _paired_tile(within // per_q, nq) * per_q + within % per_q


def _in_proj(x2d, nw, w_perm, conv_w, conv_b, tm, tq, nq):
    M, D = x2d.shape
    out_shape = [jax.ShapeDtypeStruct((M, w), dt) for w, dt in zip(_GROUP_WIDTH, _GROUP_DTYPE)]
    out_specs = [pl.BlockSpec((tm, w), lambda i: (i, 0)) for w in _GROUP_WIDTH]
    for g in (_G_AQ, _G_AZ):
        out_specs[g] = pl.BlockSpec((tm, _GROUP_WIDTH[g]), lambda i: (_paired_rows(i, tm, tq, nq), 0))
    return pl.pallas_call(
        functools.partial(_in_proj_kernel, tm=tm, tiles_per_seq=nq * tq // tm),
        grid=(M // tm,),
        in_specs=[
            pl.BlockSpec((tm, D), lambda i: (i, 0)),
            pl.BlockSpec((1, D), lambda i: (0, 0)),
            pl.BlockSpec((D, IN_COLS_PAD), lambda i: (0, 0)),
            pl.BlockSpec((CONV_K, CONV_COLS), lambda i: (0, 0)),
            pl.BlockSpec((1, DBR), lambda i: (0, 0)),
        ],
        out_specs=out_specs,
        out_shape=out_shape,
        scratch_shapes=[pltpu.VMEM((tm + 8, CONV_COLS), F32)],
        compiler_params=_cparams(("arbitrary",)),
        name="in_proj",
    )(x2d, nw, w_perm, conv_w, conv_b)


def _attn_kernel(lam_ref, sw_ref, q_ref, k_ref, v_ref, z_ref, o_ref,
                 vext_ref, qq_ref, s_ref, m_ref, acc_ref, *, tq, nq, lambda_init):
    p = pl.program_id(2)

    @pl.when(p == 0)
    def _():
        vext_ref[:, :HEAD] = v_ref[...]
        vext_ref[:, HEAD:] = jnp.ones((v_ref.shape[0], HEAD), BF16)

    lane = lax.broadcasted_iota(jnp.int32, (tq, HEAD), 1)
    for side in range(2):
        q = q_ref[side * tq:(side + 1) * tq, :]
        zero = jnp.zeros_like(q)
        qq_ref[side, :tq, :] = jnp.where(lane < HEAD // 2, q, zero)
        qq_ref[side, tq:, :] = jnp.where(lane >= HEAD // 2, q, zero)
    m_ref[...] = jnp.full(m_ref.shape, -jnp.inf, F32)
    acc_ref[...] = jnp.zeros_like(acc_ref)

    diag_a = p * tq
    diag_b = (nq - 1 - p) * tq

    def tile(n):
        if isinstance(n, int) and n < 2:
            return n, (diag_a, diag_b)[n]
        t = n - 2
        full_side = (t >= p).astype(jnp.int32)
        full_start = (t - p * full_side) * tq
        if isinstance(n, int):
            return full_side, full_start
        side = jnp.where(n < 2, n, full_side)
        return side, jnp.where(n == 0, diag_a, jnp.where(n == 1, diag_b, full_start))

    def scores(n, masked=False):
        side, start = tile(n)
        k = k_ref[pl.ds(pl.multiple_of(start, tq), tq), :]
        s = _dot_nt(qq_ref[side], k)
        if masked:
            qc = lax.broadcasted_iota(jnp.int32, (2 * tq, tq), 0)
            qc = jnp.where(qc >= tq, qc - tq, qc) // CHUNK
            kc = lax.broadcasted_iota(jnp.int32, (2 * tq, tq), 1) // CHUNK
            s = jnp.where(kc <= qc, s, NEG_INF)
        return s

    def process(buf, n):
        side, start = tile(n)
        s = s_ref[buf]
        m_old = m_ref[side]
        m_new = jnp.maximum(m_old, jnp.max(s, axis=1, keepdims=True))
        alpha = jnp.exp2(m_old - m_new)
        pr = jnp.exp2(s - jnp.concatenate([m_new] * (tq // HEAD), axis=1)).astype(BF16)
        pv = _dot(pr, vext_ref[pl.ds(pl.multiple_of(start, tq), tq), :])
        acc_ref[side] = jnp.concatenate([alpha, alpha], axis=1) * acc_ref[side] + pv
        m_ref[side] = m_new

    ntiles = nq + 1
    s_ref[0] = scores(0, masked=True)
    s_ref[1] = scores(1, masked=True)
    process(0, 0)
    s_ref[0] = scores(2)
    process(1, 1)

    def pair(u, carry):
        n = 2 * u
        s_ref[1] = scores(n + 1)
        process(0, n)
        s_ref[0] = scores(n + 2)
        process(1, n + 1)
        return carry

    npairs = ntiles // 2 - 1

    group = 7

    def pair_group(w, carry):
        for i in range(group):
            pair(group * w + 1 + i, carry)
        return carry

    lax.fori_loop(0, npairs // group + jnp.minimum(p, 0), pair_group, 0)
    lax.fori_loop(npairs - npairs % group + 1, npairs + 1 + jnp.minimum(p, 0), pair, 0)
    process(0, ntiles - 1)

    lp = lam_ref[...]
    lam = (jnp.exp(jnp.sum(lp[0:1] * lp[1:2], axis=1, keepdims=True))
           - jnp.exp(jnp.sum(lp[2:3] * lp[3:4], axis=1, keepdims=True)) + lambda_init)
    for side in range(2):
        acc = acc_ref[side]
        o1 = acc[:tq, :HEAD] / acc[:tq, HEAD:]
        o2 = acc[tq:, :HEAD] / acc[tq:, HEAD:]
        o = o1 - lam * o2
        ms = jnp.mean(o * o, axis=-1, keepdims=True)
        o = o * lax.rsqrt(ms + 1e-5) * sw_ref[...] * (1.0 - lambda_init)
        rows = slice(side * tq, (side + 1) * tq)
        o_ref[rows, :] = (o * _silu(z_ref[rows, :])).astype(o_ref.dtype)


def _diff_attn(q, k, v, z, lam_p, subln_w, lambda_init, tq):
    B, S, _ = q.shape
    nq = S // tq
    blk = pl.BlockSpec((None, 2 * tq, HEAD), lambda b, h, p: (b, p, h))
    kv = pl.BlockSpec((None, S, HEAD), lambda b, h, p: (b, 0, h))
    return pl.pallas_call(
        functools.partial(_attn_kernel, tq=tq, nq=nq, lambda_init=lambda_init),
        grid=(B, NHEAD, nq // 2),
        in_specs=[
            pl.BlockSpec((4, HEAD // 2), lambda b, h, p: (0, 0)),
            pl.BlockSpec((1, HEAD), lambda b, h, p: (0, 0)),
            blk, kv, kv, blk,
        ],
        out_specs=blk,
        out_shape=jax.ShapeDtypeStruct((B, S, DBR), BF16),
        scratch_shapes=[
            pltpu.VMEM((S, 2 * HEAD), BF16),
            pltpu.VMEM((2, 2 * tq, HEAD), BF16),
            pltpu.VMEM((2, 2 * tq, tq), F32),
            pltpu.VMEM((2, 2 * tq, HEAD), F32),
            pltpu.VMEM((2, 2 * tq, 2 * HEAD), F32),
        ],
        compiler_params=_cparams(("arbitrary", "arbitrary", "arbitrary")),
        name="diff_attn",
    )(lam_p, subln_w, q, k, v, z)


def _split_bf16(a, terms):
    out = []
    for _ in range(terms):
        piece = a.astype(BF16)
        out.append(piece)
        a = a - piece.astype(F32)
    return out


class _GdnGroup:
    def __init__(self, bi, tt, refs, consts):
        self.bi, self.tt = bi, tt
        self.qkv_ref, self.z_ref, self.ba_ref, self.alog_ref, self.dtb_ref, self.o_ref, self.state_ref = refs
        self.blk_tri, self.blk_strict, self.eye, self.tril01, self.nw = consts

    def prepare(self):
        bi, tt = self.bi, self.tt
        ba = self.ba_ref[bi]
        beta_all = _sigmoid(ba)
        g = -jnp.exp(self.alog_ref[...]) * _softplus(ba + self.dtb_ref[...])
        gcum = sum(_dot(self.tril01, piece) for piece in _split_bf16(g, 3))
        gcum_t = gcum.T
        self.qs, self.rhs, self.xs, self.nbs = [], [], [], []
        self.a_blks, self.kdts, self.decays = [], [], []
        self.o_raw = [[] for _ in range(NHEAD)]
        for h in range(NHEAD):
            qh = self.qkv_ref[bi, :, h * HEAD:(h + 1) * HEAD]
            kh = self.qkv_ref[bi, :, DBR + h * HEAD:DBR + (h + 1) * HEAD]
            vh = self.qkv_ref[bi, :, 2 * DBR + h * HEAD:2 * DBR + (h + 1) * HEAD]
            beta = jnp.broadcast_to(beta_all[:, h:h + 1], (tt, HEAD))
            gc = jnp.broadcast_to(gcum[:, NHEAD + h:NHEAD + h + 1], (tt, HEAD))
            eg = jnp.exp(gc)
            grow = gcum_t[NHEAD + h:NHEAD + h + 1, :]
            gcw = jnp.concatenate([gc] * (tt // HEAD), axis=1)
            decay = jnp.exp(jnp.where(self.blk_tri, gcw - grow, -jnp.inf))
            kb = kh * beta
            kq = _dot_nt(jnp.concatenate([qh, kb], axis=0).astype(BF16), kh.astype(BF16))
            a_bd = jnp.where(self.blk_tri, kq[:tt] * decay, 0.0).astype(BF16)
            lmat = jnp.where(self.blk_strict, kq[tt:] * decay, 0.0)
            self.rhs.append(jnp.concatenate([vh * beta, kb * eg], axis=1).astype(BF16))
            self.qs.append(qh * eg)
            blocks, kdts, decays = [], [], []
            for ch in range(tt // CHUNK):
                lo, hi = ch * CHUNK, (ch + 1) * CHUNK
                g_last = gc[hi - 1:hi]
                blocks.append(a_bd[lo:hi, lo:hi])
                kdts.append((kh[lo:hi] * jnp.exp(g_last - gc[lo:hi])).T.astype(BF16))
                decays.append(jnp.exp(g_last))
            self.a_blks.append(blocks)
            self.kdts.append(kdts)
            self.decays.append(decays)
            self.xs.append(self.eye - lmat)
            self.nbs.append((-lmat).astype(BF16))

    def inverse_level(self):
        self.nbs = [_dot(nb, nb).astype(BF16) for nb in self.nbs]
        self.xs = [x + _dot(x.astype(BF16), nb) for x, nb in zip(self.xs, self.nbs)]

    def finish_wy(self):
        self.uws = [_dot(x.astype(BF16), rh) for x, rh in zip(self.xs, self.rhs)]
        self.wqes = [jnp.concatenate([uw[:, HEAD:], qe], axis=1).astype(BF16)
                     for uw, qe in zip(self.uws, self.qs)]
        self.sts = [self.state_ref[self.bi * NHEAD + h] for h in range(NHEAD)]

    def chunk_read(self, ch):
        lo, hi = ch * CHUNK, (ch + 1) * CHUNK
        self.ws_qs = []
        for h in range(NHEAD):
            stb = self.sts[h].astype(BF16)
            self.ws_qs.append((_dot(self.wqes[h][lo:hi, :HEAD], stb), _dot(self.wqes[h][lo:hi, HEAD:], stb)))

    def chunk_update(self, ch):
        lo, hi = ch * CHUNK, (ch + 1) * CHUNK
        for h in range(NHEAD):
            w_s, q_s = self.ws_qs[h]
            v_new = (self.uws[h][lo:hi, :HEAD] - w_s).astype(BF16)
            self.o_raw[h].append(q_s + _dot(self.a_blks[h][ch], v_new))
            self.sts[h] = self.sts[h] * self.decays[h][ch] + _dot(self.kdts[h][ch], v_new)

    def finish_head(self, h):
        bi = self.bi
        o = jnp.concatenate(self.o_raw[h], axis=0)
        ms = jnp.mean(o * o, axis=-1, keepdims=True)
        o = o * lax.rsqrt(ms + 1e-6) * self.nw
        zz = self.z_ref[bi, :, h * HEAD:(h + 1) * HEAD]
        self.o_ref[bi, :, h * HEAD:(h + 1) * HEAD] = (o * _silu(zz)).astype(self.o_ref.dtype)
        self.state_ref[bi * NHEAD + h] = self.sts[h]

    def finish_stages(self):
        return [functools.partial(self.finish_head, h) for h in range(NHEAD)]


_INVERSE_LEVELS = 5


def _gdn_kernel(qkv_ref, z_ref, ba_ref, alog_ref, dtb_ref, nw_ref, o_ref, state_ref, *, tt, nb):
    t = pl.program_id(1)

    @pl.when(t == 0)
    def _():
        state_ref[...] = jnp.zeros_like(state_ref)

    r = lax.broadcasted_iota(jnp.int32, (tt, tt), 0)
    c = lax.broadcasted_iota(jnp.int32, (tt, tt), 1)
    same_chunk = (r // CHUNK) == (c // CHUNK)
    blk_tri = same_chunk & (c <= r)
    blk_strict = same_chunk & (c < r)
    consts = (blk_tri, blk_strict, (r == c).astype(F32), blk_tri.astype(BF16), nw_ref[...])
    refs = (qkv_ref, z_ref, ba_ref, alog_ref, dtb_ref, o_ref, state_ref)
    groups = [_GdnGroup(bi, tt, refs, consts) for bi in range(nb)]
    nchunk = tt // CHUNK

    def recurrence_with(group, fill):
        slots = [[] for _ in range(2 * nchunk)]
        for i, stage in enumerate(fill):
            slots[i * 2 * nchunk // len(fill)].append(stage)
        for ch in range(nchunk):
            group.chunk_read(ch)
            for stage in slots[2 * ch]:
                stage()
            group.chunk_update(ch)
            for stage in slots[2 * ch + 1]:
                stage()

    first = groups[0]
    first.prepare()
    for _ in range(_INVERSE_LEVELS):
        first.inverse_level()
    first.finish_wy()
    pending = []
    for prev, cur in zip(groups[:-1], groups[1:]):
        wy = [cur.prepare] + [cur.inverse_level] * _INVERSE_LEVELS + [cur.finish_wy]
        recurrence_with(prev, wy + pending)
        pending = prev.finish_stages()
    recurrence_with(groups[-1], pending)
    for stage in groups[-1].finish_stages():
        stage()


def _gdn(qkv, z, ba, alog_row, dtb_row, norm_w, tt, nb):
    B, S, _ = qkv.shape
    row = lambda w: pl.BlockSpec((1, w), lambda b, t: (0, 0))
    return pl.pallas_call(
        functools.partial(_gdn_kernel, tt=tt, nb=nb),
        grid=(B // nb, S // tt),
        in_specs=[
            pl.BlockSpec((nb, tt, 3 * DBR), lambda b, t: (b, t, 0)),
            pl.BlockSpec((nb, tt, DBR), lambda b, t: (b, t, 0)),
            pl.BlockSpec((nb, tt, HEAD), lambda b, t: (b, t, 0)),
            row(HEAD), row(HEAD), row(HEAD),
        ],
        out_specs=pl.BlockSpec((nb, tt, DBR), lambda b, t: (b, t, 0)),
        out_shape=jax.ShapeDtypeStruct((B, S, DBR), BF16),
        scratch_shapes=[pltpu.VMEM((nb * NHEAD, HEAD, HEAD), F32)],
        compiler_params=_cparams(("arbitrary", "arbitrary")),
        name="gdn",
    )(qkv, z, ba, alog_row, dtb_row, norm_w)


def _lru_kernel(x_ref, z_ref, gw_ref, gb_ref, lam_ref, o_ref, hprev_ref, *, tt):
    t = pl.program_id(1)

    @pl.when(t == 0)
    def _():
        hprev_ref[...] = jnp.zeros_like(hprev_ref)

    xc = x_ref[...]

    gates = [_dot(xc[:, n * HEAD:(n + 1) * HEAD].astype(BF16), gw_ref[n]) for n in range(NHEAD)]
    gi = jnp.concatenate([gt[:, :HEAD] for gt in gates], axis=1) + gb_ref[0:1, :]
    gr = jnp.concatenate([gt[:, HEAD:] for gt in gates], axis=1) + gb_ref[1:2, :]
    i_t = _sigmoid(gi)
    r_t = _sigmoid(gr)
    log_a = -LRU_C * r_t * _softplus(-lam_ref[...])
    a = jnp.exp(log_a)
    one_m_a2 = -jnp.tanh(log_a) * (a * a + 1.0)
    root = jnp.where(one_m_a2 > 0.0, one_m_a2 * lax.rsqrt(one_m_a2), 0.0)
    b = root * (i_t * xc)

    sub = lax.broadcasted_iota(jnp.int32, (tt, DBR), 0) % 8
    d = 1
    while d < 8:
        keep = sub >= d
        a_sh = pltpu.roll(a, d, 0)
        b_sh = pltpu.roll(b, d, 0)
        b = jnp.where(keep, a * b_sh + b, b)
        a = jnp.where(keep, a * a_sh, a)
        d *= 2
    carry = hprev_ref[...]
    groups = []
    for k in range(tt // 8):
        hk = b[8 * k:8 * k + 8] + a[8 * k:8 * k + 8] * carry
        groups.append(hk)
        carry = hk[7:8]
    h = jnp.concatenate(groups, axis=0)
    hprev_ref[...] = carry
    o_ref[...] = (h * _silu(z_ref[...])).astype(o_ref.dtype)


def _rglru(xc, z, gate_w, gate_b, lam, tt):
    B, S, _ = xc.shape
    blk = pl.BlockSpec((None, tt, DBR), lambda b, t: (b, t, 0))
    return pl.pallas_call(
        functools.partial(_lru_kernel, tt=tt),
        grid=(B, S // tt),
        in_specs=[
            blk, blk,
            pl.BlockSpec((NHEAD, HEAD, 2 * HEAD), lambda b, t: (0, 0, 0)),
            pl.BlockSpec((2, DBR), lambda b, t: (0, 0)),
            pl.BlockSpec((1, DBR), lambda b, t: (0, 0)),
        ],
        out_specs=blk,
        out_shape=jax.ShapeDtypeStruct((B, S, DBR), BF16),
        scratch_shapes=[pltpu.VMEM((1, DBR), F32)],
        compiler_params=_cparams(("arbitrary", "arbitrary")),
        name="rglru",
    )(xc, z, gate_w, gate_b, lam)


def _out_proj_kernel(ya_ref, yg_ref, yr_ref, w_ref, x_ref, fw_ref, o_ref, *, final_norm):
    y = x_ref[...]
    for n, y_ref in enumerate((ya_ref, yg_ref, yr_ref)):
        y = y + _dot(y_ref[...], w_ref[n * DBR:(n + 1) * DBR, :])
    if final_norm:
        ms = jnp.mean(y * y, axis=-1, keepdims=True)
        y = y * lax.rsqrt(ms + 1e-6) * fw_ref[...]
    o_ref[...] = y


def _out_proj(ya, yg, yr, w_out, x2d, fw, tm, tq, nq, final_norm):
    M, D = x2d.shape
    yblk = pl.BlockSpec((tm, DBR), lambda i: (i, 0))
    yablk = pl.BlockSpec((tm, DBR), lambda i: (_paired_rows(i, tm, tq, nq), 0))
    xblk = pl.BlockSpec((tm, D), lambda i: (i, 0))
    return pl.pallas_call(
        functools.partial(_out_proj_kernel, final_norm=final_norm),
        grid=(M // tm,),
        in_specs=[yablk, yblk, yblk,
                  pl.BlockSpec((3 * DBR, D), lambda i: (0, 0)),
                  xblk,
                  pl.BlockSpec((1, D), lambda i: (0, 0))],
        out_specs=xblk,
        out_shape=jax.ShapeDtypeStruct((M, D), F32),
        compiler_params=_cparams(("arbitrary",)),
        name="out_proj",
    )(ya, yg, yr, w_out, x2d, fw)


def _permute_w_in(w):
    d = DBR
    gba = jnp.pad(w[:, 8 * d:8 * d + 2 * NHEAD], ((0, 0), (0, HEAD - 2 * NHEAD)))
    return jnp.concatenate([w[:, :8 * d], w[:, 8 * d + 2 * NHEAD:], gba], axis=1).astype(BF16)


def _lane_row(vals, offset):
    return jnp.zeros((1, HEAD), F32).at[0, offset:offset + vals.shape[0]].set(vals.astype(F32))


def kernel(x, norm_w, w_in, attn_lambda, attn_subln_w, gdn_conv_w, gdn_a_log, gdn_dt_bias,
           gdn_norm_w, lru_conv_w, lru_conv_b, lru_gate_w, lru_gate_b, lru_log_param,
           w_out, final_norm_w):
    B, S, D = x.shape
    depth = w_in.shape[0]
    M = B * S
    tm = min(512, M)
    tq = min(512, S // 2)
    nq = S // tq
    tt = min(256, S)
    x2d = x.reshape(M, D).astype(F32)
    fw = final_norm_w.reshape(1, D).astype(F32)
    for l in range(depth):
        wp = _permute_w_in(w_in[l])
        conv_w = jnp.concatenate([gdn_conv_w[l], lru_conv_w[l]], axis=1).astype(F32)
        aq, ak, av, az, gqkv, gz, rxc, rz, gba = _in_proj(
            x2d, norm_w[l].reshape(1, D), wp, conv_w, lru_conv_b[l].reshape(1, DBR).astype(F32), tm, tq, nq)
        r3 = lambda a: a.reshape(B, S, a.shape[-1])
        lambda_init = 0.8 - 0.6 * math.exp(-0.3 * l)
        ya = _diff_attn(r3(aq), r3(ak), r3(av), r3(az), attn_lambda[l].astype(F32),
                        attn_subln_w[l].reshape(1, HEAD).astype(F32), lambda_init, tq)
        yg = _gdn(r3(gqkv), r3(gz), r3(gba),
                  _lane_row(gdn_a_log[l], NHEAD), _lane_row(gdn_dt_bias[l], NHEAD),
                  gdn_norm_w[l].reshape(1, HEAD).astype(F32), tt, 4 if B % 4 == 0 else (2 if B % 2 == 0 else 1))
        gw = jnp.concatenate([lru_gate_w[l, 0], lru_gate_w[l, 1]], axis=-1).astype(BF16)
        yr = _rglru(r3(rxc), r3(rz), gw, lru_gate_b[l].astype(F32),
                    lru_log_param[l].reshape(1, DBR).astype(F32), tt)
        x2d = _out_proj(ya.reshape(M, DBR), yg.reshape(M, DBR), yr.reshape(M, DBR),
                        w_out[l].astype(BF16), x2d, fw, tm, tq, nq, final_norm=(l == depth - 1))
    return x2d.reshape(B, S, D).astype(x.dtype)
```
